```python
import jax, jax.numpy as jnp
from jax import lax
import numpy as np

D_MODEL = 4096
BATCH = 4
SEQ = 2048
DEPTH = 2
DEC_BATCH = 128
DEC_SEQ = 4
PAST_LEN = 16384
PAGE_SIZE = 128

MLA_HEADS = 16
MLA_NOPE = 128
MLA_ROPE = 64
MLA_V = 128
MLA_Q_LORA = 768
MLA_KV_LORA = 128
ROPE_THETA = 10000.0
MLA_SCALE = (MLA_NOPE + MLA_ROPE) ** -0.5
FOX_HEADS = 32
FOX_HD = 64
FOX_SCALE = FOX_HD ** -0.5
FORGET_BIAS_INIT = 3.0
NSA_HEADS = 32
NSA_HD = 64
NSA_SCALE = NSA_HD ** -0.5
NSA_BLOCK = 64
NSA_TOPN = 16
NSA_WINDOW = 512
SEL_FORCED = 1.0e4
SEL_INVALID = -1.0
PEER_HEADS = 8
PEER_NKEYS = 128
PEER_EXPERTS = PEER_NKEYS * PEER_NKEYS
PEER_DKEY = 256
PEER_TOPK = 16
PEER_TOKEN_BLOCK = 128
QBLOCK = 128
N_BRANCH = 3
LN_EPS = 1e-5
RMS_EPS = 1e-6
DEEPNORM_ALPHA = (2 * DEPTH) ** 0.25
DEEPNORM_BETA = (8 * DEPTH) ** -0.25
MLA_CACHE_W = MLA_KV_LORA + MLA_ROPE
FOX_CACHE_W = 2 * FOX_HD + FOX_HEADS
NSA_CACHE_W = 4 * NSA_HD
WIN_STATE_W = 2 * NSA_HD
SEG_SIZES = (MLA_Q_LORA, MLA_KV_LORA, MLA_ROPE,
             FOX_HEADS * FOX_HD, FOX_HD, FOX_HD, FOX_HEADS,
             NSA_HEADS * NSA_HD, 6 * NSA_HD, N_BRANCH * NSA_HEADS,
             N_BRANCH * D_MODEL)
C_IN = sum(SEG_SIZES)

kernel_name = 'hybrid_mla_fox_nsa_peer_step'


def _split_cols(h):
    offs, acc = [], 0
    for s in SEG_SIZES[:-1]:
        acc += s
        offs.append(acc)
    return jnp.split(h, offs, axis=-1)


def _layernorm(x, g, b):
    xf = x.astype(jnp.float32)
    mu = jnp.mean(xf, -1, keepdims=True)
    var = jnp.mean(jnp.square(xf - mu), -1, keepdims=True)
    return ((xf - mu) * lax.rsqrt(var + LN_EPS) * g + b).astype(x.dtype)


def _rmsnorm(x, g):
    xf = x.astype(jnp.float32)
    return (xf * lax.rsqrt(jnp.mean(jnp.square(xf), -1, keepdims=True) + RMS_EPS) * g).astype(x.dtype)


def _rope(x, pos):
    half = x.shape[-1] // 2
    inv = jnp.power(ROPE_THETA, -jnp.arange(half, dtype=jnp.float32) / half)
    ang = pos.astype(jnp.float32)[:, None] * inv[None, :]
    shape = (1, pos.shape[0]) + (1,) * (x.ndim - 3) + (half,)
    cos, sin = jnp.cos(ang).reshape(shape), jnp.sin(ang).reshape(shape)
    xf = x.astype(jnp.float32)
    x1, x2 = xf[..., :half], xf[..., half:]
    return jnp.concatenate([x1 * cos - x2 * sin, x1 * sin + x2 * cos], -1).astype(x.dtype)


def _alibi_slopes(n):
    return jnp.power(2.0, -8.0 * (jnp.arange(n, dtype=jnp.float32) + 1.0) / n)


def _masked_softmax(s, mask):
    s = jnp.where(mask, s.astype(jnp.float32), -jnp.inf)
    m = jnp.max(s, axis=-1, keepdims=True)
    m = jnp.where(jnp.isfinite(m), m, 0.0)
    e = jnp.exp(s - m)
    return e / jnp.maximum(jnp.sum(e, -1, keepdims=True), 1e-30)


def _map_query_blocks(fn, xs):
    t = xs[0].shape[1]
    qb = QBLOCK if t % QBLOCK == 0 else t
    n = t // qb
    blocks = tuple(jnp.moveaxis(a.reshape((a.shape[0], n, qb) + a.shape[2:]), 1, 0) for a in xs)
    out = lax.map(lambda a: fn(a[0], *a[1:]), (jnp.arange(n),) + blocks)
    out = jnp.moveaxis(out, 0, 1)
    return out.reshape((out.shape[0], t) + out.shape[3:])


def _gather_pages(pool, layer, page_table):
    g = pool[layer, page_table]
    return g.reshape(page_table.shape[0], -1, pool.shape[-1])


def _mla_attention(q_lat, q_rope, c_kv, k_rope, pos0):
    kpos = jnp.arange(c_kv.shape[1])

    def block(i, ql, qr):
        qb = ql.shape[1]
        qpos = pos0 + i * qb + jnp.arange(qb)
        s = (jnp.einsum('bqhr,bsr->bhqs', ql, c_kv).astype(jnp.float32)
             + jnp.einsum('bqhe,bse->bhqs', qr, k_rope).astype(jnp.float32)) * MLA_SCALE
        p = _masked_softmax(s, kpos[None, None, None, :] <= qpos[None, None, :, None])
        return jnp.einsum('bhqs,bsr->bqhr', p.astype(c_kv.dtype), c_kv)

    return _map_query_blocks(block, (q_lat, q_rope))


def _forgetting_attention(q, k, v, logf, pos0):
    s_len = k.shape[1]
    kpos = jnp.arange(s_len)
    cum = jnp.cumsum(logf.astype(jnp.float32), axis=1)
    cum_t = jnp.transpose(cum, (0, 2, 1))
    cum_q = cum[:, s_len - q.shape[1]:]

    def block(i, qq, cq):
        qb = qq.shape[1]
        qpos = pos0 + i * qb + jnp.arange(qb)
        s = jnp.einsum('bqhd,bsd->bhqs', qq, k).astype(jnp.float32) * FOX_SCALE
        s = s + jnp.transpose(cq, (0, 2, 1))[..., None] - cum_t[:, :, None, :]
        p = _masked_softmax(s, kpos[None, None, None, :] <= qpos[None, None, :, None])
        return jnp.einsum('bhqs,bsd->bqhd', p.astype(v.dtype), v)

    return _map_query_blocks(block, (q, cum_q))


def _compress(kv, pe, w):
    b, s, d = kv.shape
    n = s // NSA_BLOCK
    blocks = kv[:, :n * NSA_BLOCK].reshape(b, n, NSA_BLOCK, d) + pe
    return jnp.einsum('bnld,lde->bne', blocks, w)


def _nsa_compressed_selected(q, k_c, v_c, k_s, v_s, pos0):
    bsz, s_len, _ = k_s.shape
    n_cmp = k_c.shape[1]
    n_cols = max(-(-s_len // NSA_BLOCK), NSA_TOPN)
    pad = ((0, 0), (0, n_cols * NSA_BLOCK - s_len), (0, 0))
    k_pad, v_pad = jnp.pad(k_s, pad), jnp.pad(v_s, pad)
    cmp_end = (jnp.arange(n_cmp) + 1) * NSA_BLOCK - 1
    blk_id = jnp.arange(n_cols)
    slopes = _alibi_slopes(NSA_HEADS)

    def block(i, qq):
        qb = qq.shape[1]
        qpos = pos0 + i * qb + jnp.arange(qb)
        dist_c = qpos[:, None] - cmp_end[None, :]
        s = (jnp.einsum('bqhd,bnd->bhqn', qq, k_c).astype(jnp.float32) * NSA_SCALE
             - slopes[:, None, None] * dist_c.astype(jnp.float32))
        p = _masked_softmax(s, (dist_c >= 0)[None, None])
        o_cmp = jnp.einsum('bhqn,bnd->bqhd', p.astype(v_c.dtype), v_c)
        imp = jnp.pad(jnp.sum(p, axis=1), ((0, 0), (0, 0), (0, n_cols - n_cmp)))
        complete = (blk_id[None, :] + 1) * NSA_BLOCK - 1 <= qpos[:, None]
        forced = (blk_id[None, :] == (qpos // NSA_BLOCK)[:, None]) | (blk_id[None, :] == 0)
        score = jnp.where(forced, SEL_FORCED, jnp.where(complete, imp, SEL_INVALID))
        val, idx = lax.top_k(score, NSA_TOPN)
        tok = (idx[..., None] * NSA_BLOCK + jnp.arange(NSA_BLOCK)).reshape(bsz, qb, NSA_TOPN * NSA_BLOCK)
        k_sel = jax.vmap(lambda kk, ii: kk[ii])(k_pad, tok)
        v_sel = jax.vmap(lambda vv, ii: vv[ii])(v_pad, tok)
        dist_s = qpos[None, :, None] - tok
        ok = (dist_s >= 0) & jnp.repeat(val >= 0, NSA_BLOCK, axis=-1)
        s2 = (jnp.einsum('bqhd,bqkd->bhqk', qq, k_sel).astype(jnp.float32) * NSA_SCALE
              - slopes[None, :, None, None] * dist_s[:, None].astype(jnp.float32))
        p2 = _masked_softmax(s2, ok[:, None])
        o_slc = jnp.einsum('bhqk,bqkd->bqhd', p2.astype(v_sel.dtype), v_sel)
        return jnp.stack([o_cmp, o_slc], axis=2)

    return _map_query_blocks(block, (q,))


def _nsa_window(q, k_band, v_band, pos0):
    kpos = pos0 - NSA_WINDOW + jnp.arange(k_band.shape[1])
    slopes = _alibi_slopes(NSA_HEADS)

    def block(i, qq):
        qb = qq.shape[1]
        qpos = pos0 + i * qb + jnp.arange(qb)
        kk = lax.dynamic_slice_in_dim(k_band, i * qb, NSA_WINDOW + qb, axis=1)
        vv = lax.dynamic_slice_in_dim(v_band, i * qb, NSA_WINDOW + qb, axis=1)
        kp = lax.dynamic_slice_in_dim(kpos, i * qb, NSA_WINDOW + qb)
        dist = qpos[:, None] - kp[None, :]
        ok = (dist >= 0) & (dist < NSA_WINDOW) & (kp[None, :] >= 0)
        s = (jnp.einsum('bqhd,bkd->bhqk', qq, kk).astype(jnp.float32) * NSA_SCALE
             - slopes[:, None, None] * dist.astype(jnp.float32))
        p = _masked_softmax(s, ok[None, None])
        return jnp.einsum('bhqk,bkd->bqhd', p.astype(vv.dtype), vv)

    return _map_query_blocks(block, (q,))


def _token_mixer(x, pos0, past_mla, past_fox, past_nsa, buf_win,
                 w_in, b_f, g_q, w_uq, g_kv, w_uk, w_uv,
                 pe_cmp_k, pe_cmp_v, w_cmp_k, w_cmp_v,
                 w_proj_a, w_proj_b, w_proj_c, w_out):
    bsz, t, dm = x.shape
    pos = pos0 + jnp.arange(t)
    mq, mkv, mkr, fq, fk, fv, ff, nq, nkv, ng, mg = _split_cols(jnp.einsum('btd,dc->btc', x, w_in))

    q = jnp.einsum('btr,rhe->bthe', _rmsnorm(mq, g_q), w_uq)
    q_lat = jnp.einsum('bthe,rhe->bthr', q[..., :MLA_NOPE], w_uk)
    q_rope = _rope(q[..., MLA_NOPE:], pos)
    new_mla = jnp.concatenate([_rmsnorm(mkv, g_kv), _rope(mkr, pos)], axis=-1)
    mla_rows = jnp.concatenate([past_mla.astype(x.dtype), new_mla], axis=1)
    o_lat = _mla_attention(q_lat, q_rope, mla_rows[..., :MLA_KV_LORA], mla_rows[..., MLA_KV_LORA:], pos0)
    o_a = jnp.einsum('bthr,rhe->bthe', o_lat, w_uv).reshape(bsz, t, MLA_HEADS * MLA_V)

    logf = jax.nn.log_sigmoid(ff.astype(jnp.float32) + b_f).astype(x.dtype)
    new_fox = jnp.concatenate([fk, fv, logf], axis=-1)
    fox_rows = jnp.concatenate([past_fox.astype(x.dtype), new_fox], axis=1)
    o_b = _forgetting_attention(fq.reshape(bsz, t, FOX_HEADS, FOX_HD), fox_rows[..., :FOX_HD],
                                fox_rows[..., FOX_HD:2 * FOX_HD], fox_rows[..., 2 * FOX_HD:], pos0)
    o_b = o_b.reshape(bsz, t, FOX_HEADS * FOX_HD)

    qn = nq.reshape(bsz, t, NSA_HEADS, NSA_HD)
    new_nsa = nkv[..., :4 * NSA_HD]
    nsa_rows = jnp.concatenate([past_nsa.astype(x.dtype), new_nsa], axis=1)
    k_c = _compress(nsa_rows[..., :NSA_HD], pe_cmp_k, w_cmp_k)
    v_c = _compress(nsa_rows[..., NSA_HD:2 * NSA_HD], pe_cmp_v, w_cmp_v)
    o_cs = _nsa_compressed_selected(qn, k_c, v_c, nsa_rows[..., 2 * NSA_HD:3 * NSA_HD],
                                    nsa_rows[..., 3 * NSA_HD:], pos0)
    win_rows = jnp.concatenate([buf_win.astype(x.dtype), nkv[..., 4 * NSA_HD:]], axis=1)
    n_keep = min(NSA_WINDOW, pos0 + t)
    new_win = win_rows[:, win_rows.shape[1] - n_keep:]
    win_band = jnp.pad(win_rows, ((0, 0), (NSA_WINDOW - buf_win.shape[1], 0), (0, 0)))
    o_w = _nsa_window(qn, win_band[..., :NSA_HD], win_band[..., NSA_HD:], pos0)
    gate = jax.nn.sigmoid(ng.astype(jnp.float32)).reshape(bsz, t, N_BRANCH, NSA_HEADS, 1)
    o_c = (gate[:, :, 0] * o_cs[:, :, 0] + gate[:, :, 1] * o_cs[:, :, 1]
           + gate[:, :, 2] * o_w).astype(x.dtype).reshape(bsz, t, NSA_HEADS * NSA_HD)

    mgate = jax.nn.sigmoid(mg.astype(jnp.float32)).reshape(bsz, t, N_BRANCH, dm)
    merged = (mgate[:, :, 0] * (o_a @ w_proj_a) + mgate[:, :, 1] * (o_b @ w_proj_b)
              + mgate[:, :, 2] * (o_c @ w_proj_c)).astype(x.dtype)
    return merged @ w_out, new_mla, new_fox, new_nsa, new_win


def _peer(x, w_pq, peer_keys, peer_u, peer_v):
    bsz, t, dm = x.shape
    xt = x.reshape(bsz * t, dm)
    n = xt.shape[0]
    xt = jnp.pad(xt, ((0, (-n) % PEER_TOKEN_BLOCK), (0, 0)))
    half = PEER_DKEY // 2

    def block(xb):
        tb = xb.shape[0]
        q = jnp.einsum('td,dhk->thk', xb, w_pq)
        s1 = jnp.einsum('thk,hnk->thn', q[..., :half], peer_keys[:, 0]).astype(jnp.float32)
        s2 = jnp.einsum('thk,hnk->thn', q[..., half:], peer_keys[:, 1]).astype(jnp.float32)
        v1, i1 = lax.top_k(s1, PEER_TOPK)
        v2, i2 = lax.top_k(s2, PEER_TOPK)
        cand = (v1[..., :, None] + v2[..., None, :]).reshape(tb, PEER_HEADS, PEER_TOPK * PEER_TOPK)
        cidx = (i1[..., :, None] * PEER_NKEYS + i2[..., None, :]).reshape(tb, PEER_HEADS, PEER_TOPK * PEER_TOPK)
        vals, sel = lax.top_k(cand, PEER_TOPK)
        eidx = jnp.take_along_axis(cidx, sel, axis=-1)
        g = jax.nn.softmax(vals, axis=-1)
        a = jnp.einsum('td,thkd->thk', xb, peer_u[eidx])
        h = (g * jax.nn.gelu(a.astype(jnp.float32), approximate=False)).astype(xb.dtype)
        return jnp.einsum('thk,thkd->td', h, peer_v[eidx])

    out = lax.map(block, xt.reshape(-1, PEER_TOKEN_BLOCK, dm)).reshape(-1, dm)[:n]
    return out.reshape(bsz, t, dm)


def setup_inputs(seed: int = 0) -> dict:
    key = jax.random.key(seed)
    keys = iter(jax.random.split(key, 32))

    def nrm(shape, scale=1.0):
        return scale * jax.random.normal(next(keys), shape, jnp.float32)

    n_pages = PAST_LEN // PAGE_SIZE
    n_used = DEC_BATCH * n_pages
    n_pool = n_used + (n_used + 3) // 4
    win_buf = min(NSA_WINDOW, PAST_LEN)
    beta = DEEPNORM_BETA
    x_prompt = nrm((BATCH, SEQ, D_MODEL))
    x_sample = nrm((DEC_BATCH, DEC_SEQ, D_MODEL))
    cache_mla = nrm((DEPTH, n_pool, PAGE_SIZE, MLA_CACHE_W))
    fox_kv = nrm((DEPTH, n_pool, PAGE_SIZE, 2 * FOX_HD))
    fox_lf = jax.nn.log_sigmoid(FORGET_BIAS_INIT + nrm((DEPTH, n_pool, PAGE_SIZE, FOX_HEADS)))
    cache_fox = jnp.concatenate([fox_kv, fox_lf], axis=-1)
    cache_nsa = nrm((DEPTH, n_pool, PAGE_SIZE, NSA_CACHE_W))
    state_nsa_win = nrm((DEPTH, DEC_BATCH, win_buf, WIN_STATE_W))
    page_table = jax.random.permutation(next(keys), n_pool)[:n_used].reshape(DEC_BATCH, n_pages).astype(jnp.int32)
    return {
        'x_prompt': x_prompt,
        'x_sample': x_sample,
        'cache_mla': cache_mla,
        'cache_fox': cache_fox,
        'cache_nsa': cache_nsa,
        'state_nsa_win': state_nsa_win,
        'page_table': page_table,
        'w_in': nrm((DEPTH, D_MODEL, C_IN), D_MODEL ** -0.5),
        'b_f': FORGET_BIAS_INIT + nrm((DEPTH, FOX_HEADS), 0.1),
        'g_q': 1.0 + nrm((DEPTH, MLA_Q_LORA), 0.02),
        'w_uq': nrm((DEPTH, MLA_Q_LORA, MLA_HEADS, MLA_NOPE + MLA_ROPE), MLA_Q_LORA ** -0.5),
        'g_kv': 1.0 + nrm((DEPTH, MLA_KV_LORA), 0.02),
        'w_uk': nrm((DEPTH, MLA_KV_LORA, MLA_HEADS, MLA_NOPE), MLA_KV_LORA ** -0.5),
        'w_uv': nrm((DEPTH, MLA_KV_LORA, MLA_HEADS, MLA_V), MLA_KV_LORA ** -0.5),
        'pe_cmp_k': nrm((DEPTH, NSA_BLOCK, NSA_HD), 0.02),
        'pe_cmp_v': nrm((DEPTH, NSA_BLOCK, NSA_HD), 0.02),
        'w_cmp_k': nrm((DEPTH, NSA_BLOCK, NSA_HD, NSA_HD), (NSA_BLOCK * NSA_HD) ** -0.5),
        'w_cmp_v': nrm((DEPTH, NSA_BLOCK, NSA_HD, NSA_HD), (NSA_BLOCK * NSA_HD) ** -0.5),
        'w_proj_a': nrm((DEPTH, MLA_HEADS * MLA_V, D_MODEL), beta * (MLA_HEADS * MLA_V) ** -0.5),
        'w_proj_b': nrm((DEPTH, FOX_HEADS * FOX_HD, D_MODEL), beta * (FOX_HEADS * FOX_HD) ** -0.5),
        'w_proj_c': nrm((DEPTH, NSA_HEADS * NSA_HD, D_MODEL), beta * (NSA_HEADS * NSA_HD) ** -0.5),
        'w_out': nrm((DEPTH, D_MODEL, D_MODEL), beta * D_MODEL ** -0.5),
        'ln1_g': 1.0 + nrm((DEPTH, D_MODEL), 0.02),
        'ln1_b': nrm((DEPTH, D_MODEL), 0.02),
        'w_pq': nrm((DEPTH, D_MODEL, PEER_HEADS, PEER_DKEY), D_MODEL ** -0.5),
        'peer_keys': nrm((DEPTH, PEER_HEADS, 2, PEER_NKEYS, PEER_DKEY // 2), (PEER_DKEY // 2) ** -0.5),
        'peer_u': nrm((DEPTH, PEER_EXPERTS, D_MODEL), D_MODEL ** -0.5),
        'peer_v': nrm((DEPTH, PEER_EXPERTS, D_MODEL), beta),
        'ln2_g': 1.0 + nrm((DEPTH, D_MODEL), 0.02),
        'ln2_b': nrm((DEPTH, D_MODEL), 0.02),
    }


def reference(x_prompt, x_sample, cache_mla, cache_fox, cache_nsa, state_nsa_win, page_table,
              w_in, b_f, g_q, w_uq, g_kv, w_uk, w_uv, pe_cmp_k, pe_cmp_v, w_cmp_k, w_cmp_v,
              w_proj_a, w_proj_b, w_proj_c, w_out, ln1_g, ln1_b,
              w_pq, peer_keys, peer_u, peer_v, ln2_g, ln2_b):
    past_len = page_table.shape[1] * cache_mla.shape[2]
    bp = x_prompt.shape[0]
    dt = x_prompt.dtype
    yp, ys = x_prompt, x_sample
    mla_p, mla_s, fox_p, fox_s, nsa_p, nsa_s, win_p, win_s = [], [], [], [], [], [], [], []
    for l in range(DEPTH):
        wts = (w_in[l], b_f[l], g_q[l], w_uq[l], g_kv[l], w_uk[l], w_uv[l],
               pe_cmp_k[l], pe_cmp_v[l], w_cmp_k[l], w_cmp_v[l],
               w_proj_a[l], w_proj_b[l], w_proj_c[l], w_out[l])
        mix_p, m_p, f_p, n_p, w_p = _token_mixer(
            yp, 0, jnp.zeros((bp, 0, MLA_CACHE_W), dt), jnp.zeros((bp, 0, FOX_CACHE_W), dt),
            jnp.zeros((bp, 0, NSA_CACHE_W), dt), jnp.zeros((bp, 0, WIN_STATE_W), dt), *wts)
        mix_s, m_s, f_s, n_s, w_s = _token_mixer(
            ys, past_len, _gather_pages(cache_mla, l, page_table), _gather_pages(cache_fox, l, page_table),
            _gather_pages(cache_nsa, l, page_table), state_nsa_win[l], *wts)
        yp = _layernorm(DEEPNORM_ALPHA * yp + mix_p, ln1_g[l], ln1_b[l])
        ys = _layernorm(DEEPNORM_ALPHA * ys + mix_s, ln1_g[l], ln1_b[l])
        yp = _layernorm(DEEPNORM_ALPHA * yp + _peer(yp, w_pq[l], peer_keys[l], peer_u[l], peer_v[l]), ln2_g[l], ln2_b[l])
        ys = _layernorm(DEEPNORM_ALPHA * ys + _peer(ys, w_pq[l], peer_keys[l], peer_u[l], peer_v[l]), ln2_g[l], ln2_b[l])
        mla_p.append(m_p); mla_s.append(m_s)
        fox_p.append(f_p); fox_s.append(f_s)
        nsa_p.append(n_p); nsa_s.append(n_s)
        win_p.append(w_p); win_s.append(w_s)
    new_mla_prompt, new_mla_sample = jnp.stack(mla_p), jnp.stack(mla_s)
    new_fox_prompt, new_fox_sample = jnp.stack(fox_p), jnp.stack(fox_s)
    new_nsa_prompt, new_nsa_sample = jnp.stack(nsa_p), jnp.stack(nsa_s)
    new_win_prompt, new_win_sample = jnp.stack(win_p), jnp.stack(win_s)
    return (yp, ys, new_mla_prompt, new_mla_sample, new_fox_prompt, new_fox_sample,
            new_nsa_prompt, new_nsa_sample, new_win_prompt, new_win_sample)
```

```python
import functools
import math

import jax
import jax.numpy as jnp
from jax import lax
from jax.experimental import pallas as pl
from jax.experimental.pallas import tpu as pltpu

ROPE_THETA = 10000.0
NSA_TOPN = 16
NSA_WINDOW = 512
SEL_FORCED = 1.0e4
SEL_INVALID = -1.0
PEER_TOPK = 16
N_BRANCH = 3
LN_EPS = 1e-5
RMS_EPS = 1e-6

LANES = 128
SUBLANES = 8
VMEM_LIMIT_BYTES = 56 * 1024 * 1024

_MXU = jnp.bfloat16
_NEG = -1e30
_F32 = jnp.float32
_SDS = jax.ShapeDtypeStruct


def _params(*sem):
    return pltpu.CompilerParams(dimension_semantics=sem, vmem_limit_bytes=VMEM_LIMIT_BYTES)


def _tile(n, target, mult=SUBLANES):
    t = min(n, target)
    t -= t % mult
    while t >= mult:
        if n % t == 0:
            return t
        t -= mult
    return n


def _round_up(n, m):
    return -(-n // m) * m


def _mm_kernel(a_ref, w_ref, o_ref):
    o_ref[...] = jnp.dot(a_ref[...], w_ref[...], preferred_element_type=_F32).astype(o_ref.dtype)


def _mm(a, w, out_dtype, tm=1088, tn=256):
    m, k = a.shape
    n = w.shape[1]
    tm, tn = _tile(m, tm), _tile(n, tn, LANES)
    return pl.pallas_call(
        _mm_kernel, grid=(m // tm, n // tn),
        in_specs=[pl.BlockSpec((tm, k), lambda i, j: (i, 0)), pl.BlockSpec((k, tn), lambda i, j: (0, j))],
        out_specs=pl.BlockSpec((tm, tn), lambda i, j: (i, j)),
        out_shape=_SDS((m, n), out_dtype), compiler_params=_params("parallel", "arbitrary"),
        name="mm")(a, w)


def _headmm_kernel(a_ref, w_ref, o_ref):
    o_ref[...] = jnp.dot(a_ref[...], w_ref[...], preferred_element_type=_F32).astype(o_ref.dtype)


def _headmm(a, w, out_dtype, tm=1088):
    m = a.shape[0]
    nh, din, dout = w.shape
    tm = _tile(m, tm)
    return pl.pallas_call(
        _headmm_kernel, grid=(m // tm, nh),
        in_specs=[pl.BlockSpec((tm, din), lambda i, h: (i, h)), pl.BlockSpec((None, din, dout), lambda i, h: (h, 0, 0))],
        out_specs=pl.BlockSpec((tm, dout), lambda i, h: (i, h)),
        out_shape=_SDS((m, nh * dout), out_dtype), compiler_params=_params("parallel", "arbitrary"),
        name="headmm")(a, w)


def _gproj_kernel(oa_ref, ob_ref, oc_ref, wa_ref, wb_ref, wc_ref, ga_ref, gb_ref, gc_ref, o_ref):
    acc = jax.nn.sigmoid(ga_ref[...]) * jnp.dot(oa_ref[...], wa_ref[...], preferred_element_type=_F32)
    acc += jax.nn.sigmoid(gb_ref[...]) * jnp.dot(ob_ref[...], wb_ref[...], preferred_element_type=_F32)
    acc += jax.nn.sigmoid(gc_ref[...]) * jnp.dot(oc_ref[...], wc_ref[...], preferred_element_type=_F32)
    o_ref[...] = acc.astype(o_ref.dtype)


def _gated_proj(o_a, o_b, o_c, w_a, w_b, w_c, h, gate_off, tm=1088, tn=256):
    m = o_a.shape[0]
    d = w_a.shape[1]
    tm, tn = _tile(m, tm), _tile(d, tn, LANES)
    nd, g0 = d // tn, gate_off // tn
    assert gate_off % tn == 0
    a_spec = lambda x: pl.BlockSpec((tm, x.shape[1]), lambda i, j: (i, 0))
    w_spec = lambda x: pl.BlockSpec((x.shape[0], tn), lambda i, j: (0, j))
    g_spec = lambda br: pl.BlockSpec((tm, tn), lambda i, j: (i, g0 + br * nd + j))
    return pl.pallas_call(
        _gproj_kernel, grid=(m // tm, nd),
        in_specs=[a_spec(o_a), a_spec(o_b), a_spec(o_c), w_spec(w_a), w_spec(w_b), w_spec(w_c),
                  g_spec(0), g_spec(1), g_spec(2)],
        out_specs=pl.BlockSpec((tm, tn), lambda i, j: (i, j)),
        out_shape=_SDS((m, d), _MXU), compiler_params=_params("parallel", "arbitrary"),
        name="gated_proj")(o_a, o_b, o_c, w_a, w_b, w_c, h, h, h)


def _add_ln_kernel(x_ref, r_ref, g_ref, b_ref, y_ref, yb_ref, *, alpha):
    v = alpha * x_ref[...] + r_ref[...]
    mu = jnp.mean(v, axis=-1, keepdims=True)
    c = v - mu
    var = jnp.mean(c * c, axis=-1, keepdims=True)
    y = c * lax.rsqrt(var + LN_EPS) * g_ref[...] + b_ref[...]
    y_ref[...] = y
    yb_ref[...] = y.astype(yb_ref.dtype)


def _add_ln(x, r, g, b, alpha, tm=256):
    m, d = x.shape
    tm = _tile(m, tm)
    row = pl.BlockSpec((tm, d), lambda i: (i, 0))
    vec = pl.BlockSpec((1, d), lambda i: (0, 0))
    return pl.pallas_call(
        functools.partial(_add_ln_kernel, alpha=alpha), grid=(m // tm,),
        in_specs=[row, row, vec, vec], out_specs=[row, row],
        out_shape=[_SDS((m, d), _F32), _SDS((m, d), _MXU)], compiler_params=_params("parallel"),
        name="add_ln")(x, r, g.reshape(1, d), b.reshape(1, d))


def _rows_from_tok(x, nh):
    tq, n = x.shape
    return jnp.broadcast_to(x[:, None, :], (tq, nh, n)).reshape(tq * nh, n)


def _rows_from_head(x, tq):
    nh, n = x.shape
    return jnp.broadcast_to(x[None, :, :], (tq, nh, n)).reshape(tq * nh, n)


def _expand_blocks(sel, blk):
    tq, nb = sel.shape
    lane_blk = lax.broadcasted_iota(jnp.int32, (tq, nb * blk), 1) // blk
    out = jnp.zeros((tq, nb * blk), _F32)
    for j in range(nb):
        out = jnp.where(lane_blk == j, sel[:, j:j + 1], out)
    return out


def _online_update(s, mask, v, m_ref, l_ref, acc_ref):
    m_old = m_ref[...]
    m_new = jnp.maximum(m_old, jnp.max(s, axis=-1, keepdims=True))
    p = jnp.exp(s - m_new)
    if mask is not None:
        p = jnp.where(mask, p, 0.0)
    alpha = jnp.exp(m_old - m_new)
    l_ref[...] = alpha * l_ref[...] + jnp.sum(p, axis=-1, keepdims=True)
    acc_ref[...] = alpha * acc_ref[...] + jnp.dot(p.astype(v.dtype), v, preferred_element_type=_F32)
    m_ref[...] = m_new


def _qk(q, k):
    return lax.dot_general(q, k, (((1,), (1,)), ((), ())), preferred_element_type=_F32)


def _attn_dense_kernel(*refs, mode, nh, tq, ck, scale, q_pos0, k_pos0, n_valid, window, blk):
    q_ref, trow_ref, slope_ref, k_ref, v_ref = refs[:5]
    extra = refs[5:-4]
    o_ref, m_ref, l_ref, acc_ref = refs[-4:]
    q0 = pl.program_id(1) * tq
    m_ref[...] = jnp.full(m_ref.shape, _NEG, _F32)
    l_ref[...] = jnp.zeros(l_ref.shape, _F32)
    acc_ref[...] = jnp.zeros(acc_ref.shape, _F32)
    q = q_ref[...]
    qpos = (q_pos0 + q0).astype(_F32) + trow_ref[...]
    last_q = q_pos0 + q0 + tq - 1
    hi = jnp.minimum((last_q - k_pos0) // ck + 1, n_valid // ck if n_valid % ck == 0 else n_valid // ck + 1)
    if window is None:
        lo = 0
    else:
        lo = jnp.maximum(q_pos0 + q0 - window + 1 - k_pos0, 0) // ck

    def body(c, carry):
        start = pl.multiple_of(c * ck, ck)
        k = k_ref[pl.ds(start, ck), :]
        v = v_ref[pl.ds(start, ck), :]
        s = _qk(q, k) * scale
        kidx = start + lax.broadcasted_iota(jnp.int32, (1, ck), 1)
        dist = qpos - (k_pos0 + kidx).astype(_F32)
        mask = (dist >= 0.0) & (kidx < n_valid)
        if mode == "fox":
            rb_ref, cb_ref = extra
            s = s + (rb_ref[...] + _rows_from_head(cb_ref[c], tq))
        elif mode in ("slc", "win"):
            s = s - slope_ref[...] * dist
            if mode == "win":
                mask = mask & (dist < float(window))
                if k_pos0 < 0:
                    mask = mask & (k_pos0 + kidx >= 0)
            else:
                (sel_ref,) = extra
                mask = mask & (_rows_from_tok(_expand_blocks(sel_ref[c], blk), nh) > 0.5)
        s = jnp.where(mask, s, _NEG)
        _online_update(s, mask, v, m_ref, l_ref, acc_ref)
        return carry

    lax.fori_loop(lo, hi, body, 0)
    o_ref[...] = (acc_ref[...] / jnp.maximum(l_ref[...], 1e-30)).astype(o_ref.dtype)


def _attn_dense(mode, q, k, v, nh, *, scale, q_pos0, k_pos0=0, n_valid=None, window=None,
                slopes=None, row_bias=None, col_bias=None, sel=None, blk=None,
                out_dtype=_F32, tq=8, ck=512):
    b, rows, dq = q.shape
    t = rows // nh
    s_len, dv = v.shape[1], v.shape[2]
    n_valid = s_len if n_valid is None else n_valid
    tq = _tile(t, tq, 1)
    ck = _tile(s_len, ck, LANES) if s_len % LANES == 0 else s_len
    nck = s_len // ck
    r = tq * nh
    trow = jnp.repeat(jnp.arange(tq, dtype=_F32), nh).reshape(r, 1)
    slope_rows = (jnp.tile(slopes, tq) if slopes is not None else jnp.zeros((r,), _F32)).reshape(r, 1)
    const = lambda shape: pl.BlockSpec(shape, lambda bi, i: (0,) * len(shape))
    in_specs = [pl.BlockSpec((None, r, dq), lambda bi, i: (bi, i, 0)), const((r, 1)), const((r, 1)),
                pl.BlockSpec((None, s_len, k.shape[2]), lambda bi, i: (bi, 0, 0)),
                pl.BlockSpec((None, s_len, dv), lambda bi, i: (bi, 0, 0))]
    args = [q, trow, slope_rows, k, v]
    if mode == "fox":
        cb = col_bias.reshape(b, nh, nck, ck).transpose(0, 2, 1, 3)
        in_specs += [pl.BlockSpec((None, r, 1), lambda bi, i: (bi, i, 0)),
                     pl.BlockSpec((None, nck, nh, ck), lambda bi, i: (bi, 0, 0, 0))]
        args += [row_bias, cb]
    elif mode == "slc":
        nb = ck // blk
        sl = sel[:, :, :nck * nb].reshape(b, t, nck, nb).transpose(0, 2, 1, 3)
        in_specs += [pl.BlockSpec((None, nck, tq, nb), lambda bi, i: (bi, 0, i, 0))]
        args += [sl]
    kern = functools.partial(_attn_dense_kernel, mode=mode, nh=nh, tq=tq, ck=ck, scale=scale, q_pos0=q_pos0,
                             k_pos0=k_pos0, n_valid=n_valid, window=window, blk=blk)
    return pl.pallas_call(
        kern, grid=(b, t // tq), in_specs=in_specs,
        out_specs=pl.BlockSpec((None, r, dv), lambda bi, i: (bi, i, 0)),
        out_shape=_SDS((b, rows, dv), out_dtype),
        scratch_shapes=[pltpu.VMEM((r, 1), _F32), pltpu.VMEM((r, 1), _F32), pltpu.VMEM((r, dv), _F32)],
        compiler_params=_params("parallel", "arbitrary"), name="attn_dense_" + mode)(*args)


def _split_hi_lo(x):
    hi = x.astype(_MXU)
    lo = (x - hi.astype(_F32)).astype(_MXU)
    return hi, lo


def _attn_paged_kernel(pt_ref, *refs, mode, nh, t_new, n_pp, page, dk, dv, scale, pos0, blk):
    del pt_ref
    q_ref, trow_ref, slope_ref = refs[:3]
    page_refs = refs[3:3 + n_pp]
    new_ref = refs[3 + n_pp]
    extra = refs[4 + n_pp:-5]
    o_ref, m_ref, l_ref, acc_ref, pref_ref = refs[-5:]
    j = pl.program_id(1)
    r = nh * t_new

    @pl.when(j == 0)
    def _():
        m_ref[...] = jnp.full(m_ref.shape, _NEG, _F32)
        l_ref[...] = jnp.zeros(l_ref.shape, _F32)
        acc_ref[...] = jnp.zeros(acc_ref.shape, _F32)
        pref_ref[...] = jnp.zeros(pref_ref.shape, _F32)

    q = q_ref[...]
    qpos = float(pos0) + trow_ref[...]
    tri = (lax.broadcasted_iota(jnp.int32, (page, page), 0)
           <= lax.broadcasted_iota(jnp.int32, (page, page), 1)).astype(_MXU)

    def fox_bias(pg, prefix):
        hi, lo = _split_hi_lo(pg[:, 2 * dk:2 * dk + nh])
        tn = (((0,), (0,)), ((), ()))
        cum = (lax.dot_general(hi, tri, tn, preferred_element_type=_F32)
               + lax.dot_general(lo, tri, tn, preferred_element_type=_F32)) + prefix
        return cum, cum[:, page - 1:page]

    def process(pages, kpos0, sel_rows, is_new):
        n = len(pages)
        scores, vals = [], []
        prefix = pref_ref[...] if mode == "fox" else None
        for pg in pages:
            k = pg[:, :dk].astype(_MXU)
            s = _qk(q, k) * scale
            if mode == "fox":
                cum, prefix = fox_bias(pg, prefix)
                s = s - _rows_from_head(cum, t_new)
                vals.append(pg[:, dk:dk + dv].astype(_MXU))
            elif mode == "slc":
                vals.append(pg[:, dk:dk + dv].astype(_MXU))
            else:
                vals.append(k[:, :dv])
            scores.append(s)
        if mode == "fox":
            pref_ref[...] = prefix
        s = scores[0] if n == 1 else jnp.concatenate(scores, axis=1)
        v = vals[0] if n == 1 else jnp.concatenate(vals, axis=0)
        kidx = lax.broadcasted_iota(jnp.int32, (1, n * page), 1)
        mask = None
        if mode == "slc":
            dist = qpos - (kpos0 + kidx).astype(_F32)
            s = s - slope_ref[...] * dist
            mask = sel_rows > 0.5
        if is_new:
            causal = kidx.astype(_F32) <= trow_ref[...]
            mask = causal if mask is None else (mask & causal)
        if mask is not None:
            s = jnp.where(mask, s, _NEG)
        _online_update(s, mask, v, m_ref, l_ref, acc_ref)

    sel_rows = None
    if mode == "slc":
        sel_ref, selnew_ref = extra
        sel_rows = _rows_from_tok(_expand_blocks(sel_ref[...], blk), nh)
    process([p_ref[...] for p_ref in page_refs], j * (n_pp * page), sel_rows, False)

    @pl.when(j == pl.num_programs(1) - 1)
    def _():
        sel_new = None
        if mode == "slc":
            sel_new = jnp.broadcast_to(selnew_ref[...], (r, page))
        process([new_ref[...]], pos0, sel_new, True)
        o_ref[...] = (acc_ref[...] / jnp.maximum(l_ref[...], 1e-30)).astype(o_ref.dtype)


def _attn_paged(mode, q, pool, layer, page_table, new_rows, nh, *, dk, dv, col_blk, width, scale, pos0,
                slopes=None, sel=None, blk=None, out_dtype=_F32, n_pp=8):
    b, rows, dq = q.shape
    t_new = rows // nh
    page = pool.shape[2]
    n_pages = page_table.shape[1]
    n_pp = _tile(n_pages, n_pp, 1)
    nj = n_pages // n_pp
    r = rows
    trow = jnp.repeat(jnp.arange(t_new, dtype=_F32), nh).reshape(r, 1)
    slope_rows = (jnp.tile(slopes, t_new) if slopes is not None else jnp.zeros((r,), _F32)).reshape(r, 1)
    new_page = jnp.pad(new_rows, ((0, 0), (0, page - t_new), (0, 0)))
    const = lambda shape: pl.BlockSpec(shape, lambda bi, j, pt: (0,) * len(shape))
    in_specs = [pl.BlockSpec((None, r, dq), lambda bi, j, pt: (bi, 0, 0)), const((r, 1)), const((r, 1))]
    in_specs += [pl.BlockSpec((None, None, page, width),
                              functools.partial(lambda p, bi, j, pt: (layer, pt[bi, j * n_pp + p], 0, col_blk), p))
                 for p in range(n_pp)]
    in_specs += [pl.BlockSpec((None, page, width), lambda bi, j, pt: (bi, 0, 0))]
    args = [q, trow, slope_rows] + [pool] * n_pp + [new_page]
    if mode == "slc":
        nb = n_pp * page // blk
        sl = sel[:, :, :nj * nb].reshape(b, t_new, nj, nb).transpose(0, 2, 1, 3)
        sel_new = jnp.repeat(sel[:, :, nj * nb], nh, axis=1).reshape(b, r, 1)
        in_specs += [pl.BlockSpec((None, None, t_new, nb), lambda bi, j, pt: (bi, j, 0, 0)),
                     pl.BlockSpec((None, r, 1), lambda bi, j, pt: (bi, 0, 0))]
        args += [sl, sel_new]
    kern = functools.partial(_attn_paged_kernel, mode=mode, nh=nh, t_new=t_new, n_pp=n_pp, page=page, dk=dk, dv=dv,
                             scale=scale, pos0=pos0, blk=blk)
    grid_spec = pltpu.PrefetchScalarGridSpec(
        num_scalar_prefetch=1, grid=(b, nj), in_specs=in_specs,
        out_specs=pl.BlockSpec((None, r, dv), lambda bi, j, pt: (bi, 0, 0)),
        scratch_shapes=[pltpu.VMEM((r, 1), _F32), pltpu.VMEM((r, 1), _F32), pltpu.VMEM((r, dv), _F32),
                        pltpu.VMEM((nh, 1), _F32)])
    return pl.pallas_call(
        kern, grid_spec=grid_spec, out_shape=_SDS((b, r, dv), out_dtype),
        compiler_params=_params("parallel", "arbitrary"), name="attn_paged_" + mode)(page_table, *args)


def _compress_kernel(x_ref, pe_ref, w_ref, o_ref, xs_ref, *, blk):
    n = x_ref.shape[0] // blk
    wd = x_ref.shape[1]
    for l in range(blk):
        rows = x_ref[pl.ds(l, n, stride=blk), :]
        xs_ref[:, l * wd:(l + 1) * wd] = (rows + pe_ref[l:l + 1, :]).astype(xs_ref.dtype)
    o_ref[...] = jnp.dot(xs_ref[...], w_ref[...], preferred_element_type=_F32)


def _compress(x, pe_kv, w_kv, blk, rows_per_step, row0=0, rows=None):
    rows = x.shape[0] if rows is None else rows
    wd = pe_kv.shape[1]
    rb = _tile(rows, rows_per_step, blk * 16)
    n = rb // blk
    assert rows % blk == 0 and row0 % rb == 0
    i0 = row0 // rb
    return pl.pallas_call(
        functools.partial(_compress_kernel, blk=blk), grid=(rows // rb,),
        in_specs=[pl.BlockSpec((rb, wd), lambda i: (i0 + i, 0)), pl.BlockSpec(pe_kv.shape, lambda i: (0, 0)),
                  pl.BlockSpec(w_kv.shape, lambda i: (0, 0))],
        out_specs=pl.BlockSpec((n, wd), lambda i: (i, 0)),
        out_shape=_SDS((rows // blk, wd), _F32),
        scratch_shapes=[pltpu.VMEM((n, blk * wd), _MXU)],
        compiler_params=_params("parallel"), name="nsa_compress")(x, pe_kv, w_kv)


def _cmp_select_kernel(q_ref, trow_ref, slope_ref, kc_ref, vc_ref, o_ref, sel_ref, *,
                       nh, tq, scale, q_pos0, n_cmp, n_cols, blk, topn):
    q0 = pl.program_id(1) * tq
    ncp = kc_ref.shape[0]
    qpos = (q_pos0 + q0).astype(_F32) + trow_ref[...]
    cidx = lax.broadcasted_iota(jnp.int32, (1, ncp), 1)
    cmp_end = ((cidx + 1) * blk - 1).astype(_F32)
    dist = qpos - cmp_end
    mask = (dist >= 0.0) & (cidx < n_cmp)
    s = _qk(q_ref[...], kc_ref[...]) * scale - slope_ref[...] * dist
    s = jnp.where(mask, s, _NEG)
    m = jnp.max(s, axis=-1, keepdims=True)
    e = jnp.where(mask, jnp.exp(s - m), 0.0)
    p = e / jnp.maximum(jnp.sum(e, axis=-1, keepdims=True), 1e-30)
    o_ref[...] = jnp.dot(p.astype(vc_ref.dtype), vc_ref[...], preferred_element_type=_F32)
    imp = jnp.sum(p.reshape(tq, nh, ncp), axis=1)
    qp = (q_pos0 + q0 + lax.broadcasted_iota(jnp.int32, (tq, 1), 0)).astype(_F32)
    blk_lo = (cidx * blk).astype(_F32)
    forced = ((blk_lo <= qp) & (qp <= cmp_end)) | (cidx == 0)
    score = jnp.where(forced, SEL_FORCED, jnp.where(cmp_end <= qp, imp, SEL_INVALID))
    score = jnp.where(cidx < n_cols, score, 2.0 * SEL_INVALID)
    rank = jnp.zeros((tq, ncp), _F32)
    for j in range(n_cols):
        c = score[:, j:j + 1]
        rank = rank + jnp.where((c > score) | ((c == score) & (cidx > j)), 1.0, 0.0)
    sel_ref[...] = jnp.where((rank < float(topn)) & (score >= 0.0), 1.0, 0.0)


def _cmp_select(q, kc, vc, nh, slopes, *, scale, q_pos0, n_cmp, n_cols, blk, topn, tq=8):
    b, rows, d = q.shape
    t = rows // nh
    ncp = kc.shape[1]
    tq = _tile(t, tq, 1)
    r = tq * nh
    trow = jnp.repeat(jnp.arange(tq, dtype=_F32), nh).reshape(r, 1)
    slope_rows = jnp.tile(slopes, tq).reshape(r, 1)
    const = lambda shape: pl.BlockSpec(shape, lambda bi, i: (0,) * len(shape))
    kern = functools.partial(_cmp_select_kernel, nh=nh, tq=tq, scale=scale, q_pos0=q_pos0, n_cmp=n_cmp,
                             n_cols=n_cols, blk=blk, topn=topn)
    return pl.pallas_call(
        kern, grid=(b, t // tq),
        in_specs=[pl.BlockSpec((None, r, d), lambda bi, i: (bi, i, 0)), const((r, 1)), const((r, 1)),
                  pl.BlockSpec((None, ncp, d), lambda bi, i: (bi, 0, 0)),
                  pl.BlockSpec((None, ncp, d), lambda bi, i: (bi, 0, 0))],
        out_specs=[pl.BlockSpec((None, r, d), lambda bi, i: (bi, i, 0)),
                   pl.BlockSpec((None, tq, ncp), lambda bi, i: (bi, i, 0))],
        out_shape=[_SDS((b, rows, d), _F32), _SDS((b, t, ncp), _F32)],
        compiler_params=_params("parallel", "arbitrary"), name="nsa_cmp_select")(q, trow, slope_rows, kc, vc)


def _top_rows(x, k):
    n = x.shape[0]
    ridx = lax.broadcasted_iota(jnp.int32, x.shape, 0)
    out = []
    for _ in range(k):
        m = jnp.max(x, axis=0, keepdims=True)
        first = jnp.min(jnp.where(x == m, ridx, n), axis=0, keepdims=True)
        x = jnp.where(ridx == first, -jnp.inf, x)
        out.append(m)
    return out


def _peer_route_kernel(q_ref, keys_ref, s1_ref, s2_ref, st_ref, *, nh, half, topk):
    tm = q_ref.shape[0]
    for h in range(nh):
        q1 = q_ref[:, 2 * h * half:(2 * h + 1) * half]
        q2 = q_ref[:, (2 * h + 1) * half:(2 * h + 2) * half]
        s1 = _qk(keys_ref[h, 0], q1)
        s2 = _qk(keys_ref[h, 1], q2)
        s1_ref[h] = s1
        s2_ref[h] = s2
        v1 = _top_rows(s1, topk)
        v2 = jnp.concatenate(_top_rows(s2, topk), axis=0)
        cand = jnp.concatenate([v1[a] + v2 for a in range(topk)], axis=0)
        vals = _top_rows(cand, topk)
        z = vals[0] * 0.0
        for a in range(topk):
            z = z + jnp.exp(vals[a] - vals[0])
        st_ref[h] = jnp.concatenate([vals[topk - 1], v1[0], v2[0:1], 1.0 / z,
                                     jnp.zeros((SUBLANES - 4, tm), _F32)], axis=0)


def _peer_route(q, keys, topk, tm=256):
    m = q.shape[0]
    nh, _, nk, half = keys.shape
    tm = _tile(m, tm, LANES)
    blk3 = lambda rows: pl.BlockSpec((nh, rows, tm), lambda i: (0, 0, i))
    return pl.pallas_call(
        functools.partial(_peer_route_kernel, nh=nh, half=half, topk=topk), grid=(m // tm,),
        in_specs=[pl.BlockSpec((tm, q.shape[1]), lambda i: (i, 0)),
                  pl.BlockSpec(keys.shape, lambda i: (0, 0, 0, 0))],
        out_specs=[blk3(nk), blk3(nk), blk3(SUBLANES)],
        out_shape=[_SDS((nh, nk, m), _F32), _SDS((nh, nk, m), _F32), _SDS((nh, SUBLANES, m), _F32)],
        compiler_params=_params("parallel"), name="peer_route")(q, keys)


def _gelu(x):
    return 0.5 * x * (1.0 + lax.erf(x * (1.0 / math.sqrt(2.0))))


def _peer_dense_kernel(x_ref, u_ref, v_ref, s1_ref, s2_ref, st_ref, o_ref, e1_ref, e2_ref, *, nh, nk):
    c = pl.program_id(1)
    tc = u_ref.shape[0]
    n_i1 = tc // nk

    @pl.when(c == 0)
    def _():
        for h in range(nh):
            st = st_ref[h]
            e1_ref[h] = jnp.exp(s1_ref[h] - st[1:2]) * st[3:4]
            e2_ref[h] = jnp.exp(s2_ref[h] - st[2:3])

    parts = []
    for ii in range(n_i1):
        i1 = c * n_i1 + ii
        g = None
        for h in range(nh):
            s1row = s1_ref[h, pl.ds(i1, 1), :]
            e1row = e1_ref[h, pl.ds(i1, 1), :]
            hit = (s1row + s2_ref[h]) >= st_ref[h, 0:1, :]
            w = jnp.where(hit, e2_ref[h] * e1row, 0.0)
            g = w if g is None else g + w
        parts.append(g)
    gate = parts[0] if n_i1 == 1 else jnp.concatenate(parts, axis=0)
    act = _qk(u_ref[...], x_ref[...])
    hid = (_gelu(act) * gate).astype(v_ref.dtype)
    upd = lax.dot_general(hid, v_ref[...], (((0,), (0,)), ((), ())), preferred_element_type=_F32)

    @pl.when(c == 0)
    def _():
        o_ref[...] = upd

    @pl.when(c != 0)
    def _():
        o_ref[...] += upd


def _peer_dense(x, u, v, s1, s2, st, tm=256, tc=256):
    m, d = x.shape
    n_exp = u.shape[0]
    nh, nk, _ = s1.shape
    tm = _tile(m, tm, LANES)
    tc = _tile(n_exp, tc, nk)
    tok3 = lambda rows: pl.BlockSpec((nh, rows, tm), lambda i, c: (0, 0, i))
    return pl.pallas_call(
        functools.partial(_peer_dense_kernel, nh=nh, nk=nk), grid=(m // tm, n_exp // tc),
        in_specs=[pl.BlockSpec((tm, d), lambda i, c: (i, 0)), pl.BlockSpec((tc, d), lambda i, c: (c, 0)),
                  pl.BlockSpec((tc, d), lambda i, c: (c, 0)), tok3(nk), tok3(nk), tok3(SUBLANES)],
        out_specs=pl.BlockSpec((tm, d), lambda i, c: (i, 0)),
        out_shape=_SDS((m, d), _F32),
        scratch_shapes=[pltpu.VMEM((nh, nk, tm), _F32), pltpu.VMEM((nh, nk, tm), _F32)],
        compiler_params=_params("parallel", "arbitrary"), name="peer_dense")(x, u, v, s1, s2, st)


def _rmsnorm(x, g):
    return x * lax.rsqrt(jnp.mean(jnp.square(x), -1, keepdims=True) + RMS_EPS) * g


def _rope(x, pos):
    half = x.shape[-1] // 2
    inv = jnp.power(ROPE_THETA, -jnp.arange(half, dtype=_F32) / half)
    ang = pos.astype(_F32)[:, None] * inv[None, :]
    shape = (pos.shape[0],) + (1,) * (x.ndim - 2) + (half,)
    cos, sin = jnp.cos(ang).reshape(shape), jnp.sin(ang).reshape(shape)
    x1, x2 = x[..., :half], x[..., half:]
    return jnp.concatenate([x1 * cos - x2 * sin, x1 * sin + x2 * cos], -1)


def _alibi_slopes(n):
    return jnp.power(2.0, -8.0 * (jnp.arange(n, dtype=_F32) + 1.0) / n)


def _pad_rows(x, n):
    return jnp.pad(x, ((0, 0), (0, n - x.shape[1]), (0, 0)))


def kernel(x_prompt, x_sample, cache_mla, cache_fox, cache_nsa, state_nsa_win, page_table, w_in, b_f, g_q, w_uq, g_kv, w_uk, w_uv, pe_cmp_k, pe_cmp_v, w_cmp_k, w_cmp_v, w_proj_a, w_proj_b, w_proj_c, w_out, ln1_g, ln1_b, w_pq, peer_keys, peer_u, peer_v, ln2_g, ln2_b):
    bp, tp, dm = x_prompt.shape
    bs, ts, _ = x_sample.shape
    depth = w_in.shape[0]
    page = cache_mla.shape[2]
    past = page_table.shape[1] * page
    n_pool = cache_mla.shape[1]
    q_lora, mla_h, mla_qd = w_uq.shape[1:]
    kv_lora, _, nope = w_uk.shape[1:]
    rope_d = mla_qd - nope
    mla_v = w_uv.shape[3]
    fox_h = b_f.shape[1]
    fox_d = (cache_fox.shape[3] - fox_h) // 2
    nsa_d = cache_nsa.shape[3] // 4
    nsa_h = w_proj_c.shape[1] // nsa_d
    blk = pe_cmp_k.shape[1]
    peer_h, dkey = w_pq.shape[2:]
    n_exp = peer_u.shape[1]
    alpha = (2 * depth) ** 0.25
    mp, ms = bp * tp, bs * ts

    seg = (q_lora, kv_lora, rope_d, fox_h * fox_d, fox_d, fox_d, fox_h, nsa_h * nsa_d, 6 * nsa_d, N_BRANCH * nsa_h,
           N_BRANCH * dm)
    src = [0]
    for s_ in seg:
        src.append(src[-1] + s_)
    groups = ((0, 1), (1, 2), (2, 3), (3, 4), (4, 6), (6, 7), (7, 8), (8, 9), (9, 10))
    off, pieces, cur = {}, [], 0
    for a, z in groups:
        width = src[z] - src[a]
        off[a] = cur
        pieces.append((src[a], src[z], _round_up(width, LANES) - width))
        cur += _round_up(width, LANES)
    gate_off = _round_up(cur, 4 * LANES)
    pieces[-1] = (pieces[-1][0], pieces[-1][1], pieces[-1][2] + gate_off - cur)
    pieces.append((src[10], src[11], 0))

    def layout_w_in(w):
        cols = []
        for a, z, padw in pieces:
            cols.append(w[:, a:z].astype(_MXU))
            if padw:
                cols.append(jnp.zeros((w.shape[0], padw), _MXU))
        return jnp.concatenate(cols, axis=1)

    pos_p = jnp.arange(tp)
    pos_s = past + jnp.arange(ts)
    pos_all = jnp.concatenate([jnp.tile(pos_p, bp), jnp.tile(pos_s, bs)])
    slopes = _alibi_slopes(nsa_h)
    n_cmp_p, n_cmp_s = tp // blk, (past + ts) // blk
    n_cols_p = max(-(-tp // blk), NSA_TOPN)
    n_cols_s = max(-(-(past + ts) // blk), NSA_TOPN)
    win_buf = state_nsa_win.shape[2]

    x = jnp.concatenate([x_prompt.reshape(mp, dm), x_sample.reshape(ms, dm)], axis=0)
    xb = x.astype(_MXU)
    outs = {k: [] for k in ("mla_p", "mla_s", "fox_p", "fox_s", "nsa_p", "nsa_s", "win_p", "win_s")}

    for l in range(depth):
        h = _mm(xb, layout_w_in(w_in[l]), _F32)
        col = lambda a, width: h[:, off[a]:off[a] + width]
        mq, mkv, mkr = col(0, q_lora), col(1, kv_lora), col(2, rope_d)
        fq, fkv, ff = col(3, fox_h * fox_d), col(4, 2 * fox_d), col(6, fox_h)
        nq, nkv, ng = col(7, nsa_h * nsa_d), col(8, 6 * nsa_d), col(9, N_BRANCH * nsa_h)

        q = _mm(_rmsnorm(mq, g_q[l]).astype(_MXU), w_uq[l].reshape(q_lora, mla_h * mla_qd).astype(_MXU), _F32)
        q = q.reshape(-1, mla_h, mla_qd)
        q_lat = _headmm(q[..., :nope].reshape(-1, mla_h * nope).astype(_MXU),
                        jnp.transpose(w_uk[l], (1, 2, 0)).astype(_MXU), _MXU)
        q_eff = jnp.concatenate([q_lat.reshape(-1, mla_h, kv_lora),
                                 _rope(q[..., nope:], pos_all).astype(_MXU)], axis=-1)
        new_mla = jnp.concatenate([_rmsnorm(mkv, g_kv[l]), _rope(mkr, pos_all)], axis=-1)
        mla_scale = float(mla_qd) ** -0.5
        kp = new_mla[:mp].reshape(bp, tp, -1).astype(_MXU)
        o_p = _attn_dense("mla", q_eff[:mp].reshape(bp, tp * mla_h, -1), kp, kp[..., :kv_lora], mla_h,
                          scale=mla_scale, q_pos0=0, out_dtype=_MXU, tq=_tile(tp, max(1, 256 // mla_h), 1))
        o_s = _attn_paged("mla", q_eff[mp:].reshape(bs, ts * mla_h, -1), cache_mla, l, page_table,
                          new_mla[mp:].reshape(bs, ts, -1), mla_h, dk=kv_lora + rope_d, dv=kv_lora, col_blk=0,
                          width=cache_mla.shape[3], scale=mla_scale, pos0=past, out_dtype=_MXU)
        o_lat = jnp.concatenate([o_p.reshape(mp, -1), o_s.reshape(ms, -1)], axis=0)
        o_a = _headmm(o_lat, jnp.transpose(w_uv[l], (1, 0, 2)).astype(_MXU), _MXU)

        logf = jax.nn.log_sigmoid(ff + b_f[l])
        new_fox = jnp.concatenate([fkv, logf], axis=-1)
        fox_scale = float(fox_d) ** -0.5
        fqb = fq.astype(_MXU)
        cum = jnp.cumsum(logf[:mp].reshape(bp, tp, fox_h), axis=1)
        o_p = _attn_dense("fox", fqb[:mp].reshape(bp, tp * fox_h, fox_d),
                          fkv[:mp, :fox_d].reshape(bp, tp, fox_d).astype(_MXU),
                          fkv[:mp, fox_d:].reshape(bp, tp, fox_d).astype(_MXU), fox_h, scale=fox_scale, q_pos0=0,
                          row_bias=cum.reshape(bp, tp * fox_h, 1), col_bias=-jnp.transpose(cum, (0, 2, 1)),
                          out_dtype=_MXU)
        o_s = _attn_paged("fox", fqb[mp:].reshape(bs, ts * fox_h, fox_d), cache_fox, l, page_table,
                          new_fox[mp:].reshape(bs, ts, -1), fox_h, dk=fox_d, dv=fox_d, col_blk=0,
                          width=cache_fox.shape[3], scale=fox_scale, pos0=past, out_dtype=_MXU)
        o_b = jnp.concatenate([o_p.reshape(mp, -1), o_s.reshape(ms, -1)], axis=0)

        nsa_scale = float(nsa_d) ** -0.5
        nqb = nq.astype(_MXU)
        new_nsa = nkv[:, :4 * nsa_d]
        pe_kv = jnp.concatenate([pe_cmp_k[l], pe_cmp_v[l]], axis=-1)
        zero = jnp.zeros((blk, nsa_d, nsa_d), _F32)
        w_kv = jnp.concatenate([jnp.concatenate([w_cmp_k[l], zero], axis=-1),
                                jnp.concatenate([zero, w_cmp_v[l]], axis=-1)], axis=1)
        w_kv = w_kv.reshape(blk * 2 * nsa_d, 2 * nsa_d).astype(_MXU)
        cmp_p = _compress(new_nsa[:mp], pe_kv, w_kv, blk, 8192).reshape(bp, n_cmp_p, 2 * nsa_d)
        ncp_p = _round_up(n_cols_p, LANES)
        q_p = nqb[:mp].reshape(bp, tp * nsa_h, nsa_d)
        o_cmp_p, sel_p = _cmp_select(q_p, _pad_rows(cmp_p[..., :nsa_d], ncp_p).astype(_MXU),
                                     _pad_rows(cmp_p[..., nsa_d:], ncp_p).astype(_MXU), nsa_h, slopes,
                                     scale=nsa_scale, q_pos0=0, n_cmp=n_cmp_p, n_cols=n_cols_p, blk=blk, topn=NSA_TOPN)
        kv_p = nkv[:mp].reshape(bp, tp, 6 * nsa_d).astype(_MXU)
        o_slc_p = _attn_dense("slc", q_p, kv_p[..., 2 * nsa_d:3 * nsa_d], kv_p[..., 3 * nsa_d:4 * nsa_d], nsa_h,
                              scale=nsa_scale, q_pos0=0, slopes=slopes, sel=sel_p, blk=blk)
        o_win_p = _attn_dense("win", q_p, kv_p[..., 4 * nsa_d:5 * nsa_d], kv_p[..., 5 * nsa_d:], nsa_h,
                              scale=nsa_scale, q_pos0=0, slopes=slopes, window=NSA_WINDOW)
        cmp_pool = _compress(cache_nsa.reshape(depth * n_pool * page, -1), pe_kv, w_kv, blk, 64 * page,
                             row0=l * n_pool * page, rows=n_pool * page)
        cmp_s = cmp_pool.reshape(n_pool, page // blk, 2 * nsa_d)[page_table].reshape(bs, -1, 2 * nsa_d)[:, :n_cmp_s]
        ncp_s = _round_up(n_cols_s, LANES)
        q_s = nqb[mp:].reshape(bs, ts * nsa_h, nsa_d)
        o_cmp_s, sel_s = _cmp_select(q_s, _pad_rows(cmp_s[..., :nsa_d], ncp_s).astype(_MXU),
                                     _pad_rows(cmp_s[..., nsa_d:], ncp_s).astype(_MXU), nsa_h, slopes,
                                     scale=nsa_scale, q_pos0=past, n_cmp=n_cmp_s, n_cols=n_cols_s, blk=blk,
                                     topn=NSA_TOPN, tq=ts)
        o_slc_s = _attn_paged("slc", q_s, cache_nsa, l, page_table, new_nsa[mp:, 2 * nsa_d:].reshape(bs, ts, -1),
                              nsa_h, dk=nsa_d, dv=nsa_d, col_blk=1, width=2 * nsa_d, scale=nsa_scale, pos0=past,
                              slopes=slopes, sel=sel_s, blk=blk)
        win_rows = jnp.concatenate([state_nsa_win[l], nkv[mp:, 4 * nsa_d:].reshape(bs, ts, 2 * nsa_d)], axis=1)
        band = _pad_rows(win_rows, _round_up(win_buf + ts, LANES)).astype(_MXU)
        o_win_s = _attn_dense("win", q_s, band[..., :nsa_d], band[..., nsa_d:], nsa_h, scale=nsa_scale,
                              q_pos0=past, k_pos0=past - win_buf, n_valid=win_buf + ts, slopes=slopes,
                              window=NSA_WINDOW, tq=ts)
        gate = jax.nn.sigmoid(ng).reshape(-1, N_BRANCH, nsa_h, 1)
        br = lambda p_, s_: jnp.concatenate([p_.reshape(mp, nsa_h, nsa_d), s_.reshape(ms, nsa_h, nsa_d)], axis=0)
        o_c = (gate[:, 0] * br(o_cmp_p, o_cmp_s) + gate[:, 1] * br(o_slc_p, o_slc_s)
               + gate[:, 2] * br(o_win_p, o_win_s)).reshape(-1, nsa_h * nsa_d).astype(_MXU)

        merged = _gated_proj(o_a, o_b, o_c, w_proj_a[l].astype(_MXU), w_proj_b[l].astype(_MXU),
                             w_proj_c[l].astype(_MXU), h, gate_off)
        mix = _mm(merged, w_out[l].astype(_MXU), _F32)
        x, xb = _add_ln(x, mix, ln1_g[l], ln1_b[l], alpha)

        pq = _mm(xb, w_pq[l].reshape(dm, peer_h * dkey).astype(_MXU), _MXU)
        s1, s2, st = _peer_route(pq, peer_keys[l].astype(_MXU), PEER_TOPK)
        ffn = _peer_dense(xb, peer_u[l].astype(_MXU), peer_v[l].astype(_MXU), s1, s2, st)
        x, xb = _add_ln(x, ffn, ln2_g[l], ln2_b[l], alpha)

        keep_p = min(NSA_WINDOW, tp)
        keep_s = min(NSA_WINDOW, past + ts)
        outs["mla_p"].append(new_mla[:mp].reshape(bp, tp, -1))
        outs["mla_s"].append(new_mla[mp:].reshape(bs, ts, -1))
        outs["fox_p"].append(new_fox[:mp].reshape(bp, tp, -1))
        outs["fox_s"].append(new_fox[mp:].reshape(bs, ts, -1))
        outs["nsa_p"].append(new_nsa[:mp].reshape(bp, tp, -1))
        outs["nsa_s"].append(new_nsa[mp:].reshape(bs, ts, -1))
        outs["win_p"].append(nkv[:mp, 4 * nsa_d:].reshape(bp, tp, -1)[:, tp - keep_p:])
        outs["win_s"].append(win_rows[:, win_rows.shape[1] - keep_s:])

    return (x[:mp].reshape(bp, tp, dm), x[mp:].reshape(bs, ts, dm),
            jnp.stack(outs["mla_p"]), jnp.stack(outs["mla_s"]), jnp.stack(outs["fox_p"]), jnp.stack(outs["fox_s"]),
            jnp.stack(outs["nsa_p"]), jnp.stack(outs["nsa_s"]), jnp.stack(outs["win_p"]), jnp.stack(outs["win_s"]))
```

```python
import functools
import math

import jax
import jax.numpy as jnp
from jax import lax
from jax.experimental import pallas as pl
from jax.experimental.pallas import tpu as pltpu

ROPE_THETA = 10000.0
NSA_TOPN = 16
NSA_WINDOW = 512
SEL_FORCED = 1.0e4
SEL_INVALID = -1.0
PEER_TOPK = 16
N_BRANCH = 3
LN_EPS = 1e-5
RMS_EPS = 1e-6

LANES = 128
SUBLANES = 8
VMEM_LIMIT_BYTES = 56 * 1024 * 1024

_MXU = jnp.bfloat16
_NEG = -1e30
_F32 = jnp.float32
_SDS = jax.ShapeDtypeStruct


def _params(*sem):
    return pltpu.CompilerParams(dimension_semantics=sem, vmem_limit_bytes=VMEM_LIMIT_BYTES)


def _tile(n, target, mult=SUBLANES):
    t = min(n, target)
    t -= t % mult
    while t >= mult:
        if n % t == 0:
            return t
        t -= mult
    return n


def _round_up(n, m):
    return -(-n // m) * m


def _mm_kernel(a_ref, w_ref, o_ref):
    o_ref[...] = jnp.dot(a_ref[...], w_ref[...], preferred_element_type=_F32).astype(o_ref.dtype)


def _mm(a, w, out_dtype, tm=1088, tn=256):
    m, k = a.shape
    n = w.shape[1]
    tm, tn = _tile(m, tm), _tile(n, tn, LANES)
    return pl.pallas_call(
        _mm_kernel, grid=(m // tm, n // tn),
        in_specs=[pl.BlockSpec((tm, k), lambda i, j: (i, 0)), pl.BlockSpec((k, tn), lambda i, j: (0, j))],
        out_specs=pl.BlockSpec((tm, tn), lambda i, j: (i, j)),
        out_shape=_SDS((m, n), out_dtype), compiler_params=_params("parallel", "arbitrary"),
        name="mm")(a, w)


def _mm_acc_kernel(a_ref, w_ref, o_ref):
    upd = jnp.dot(a_ref[...], w_ref[...], preferred_element_type=_F32)

    @pl.when(pl.program_id(2) == 0)
    def _():
        o_ref[...] = upd

    @pl.when(pl.program_id(2) != 0)
    def _():
        o_ref[...] += upd


def _mm_acc(a, w, tm=1088, tn=2048, tk=512):
    m, k = a.shape
    n = w.shape[1]
    tm, tn, tk = _tile(m, tm), _tile(n, tn, LANES), _tile(k, tk, LANES)
    return pl.pallas_call(
        _mm_acc_kernel, grid=(m // tm, n // tn, k // tk),
        in_specs=[pl.BlockSpec((tm, tk), lambda i, j, kk: (i, kk)), pl.BlockSpec((tk, tn), lambda i, j, kk: (kk, j))],
        out_specs=pl.BlockSpec((tm, tn), lambda i, j, kk: (i, j)),
        out_shape=_SDS((m, n), _F32), compiler_params=_params("parallel", "parallel", "arbitrary"),
        name="mm_acc")(a, w)


def _headmm_kernel(a_ref, w_ref, o_ref):
    o_ref[...] = jnp.dot(a_ref[...], w_ref[...], preferred_element_type=_F32).astype(o_ref.dtype)


def _headmm(a, w, out_dtype, tm=1088):
    m = a.shape[0]
    nh, din, dout = w.shape
    tm = _tile(m, tm)
    return pl.pallas_call(
        _headmm_kernel, grid=(m // tm, nh),
        in_specs=[pl.BlockSpec((tm, din), lambda i, h: (i, h)), pl.BlockSpec((None, din, dout), lambda i, h: (h, 0, 0))],
        out_specs=pl.BlockSpec((tm, dout), lambda i, h: (i, h)),
        out_shape=_SDS((m, nh * dout), out_dtype), compiler_params=_params("parallel", "arbitrary"),
        name="headmm")(a, w)


def _gproj_kernel(oa_ref, ob_ref, oc_ref, wa_ref, wb_ref, wc_ref, ga_ref, gb_ref, gc_ref, o_ref):
    acc = jax.nn.sigmoid(ga_ref[...]) * jnp.dot(oa_ref[...], wa_ref[...], preferred_element_type=_F32)
    acc += jax.nn.sigmoid(gb_ref[...]) * jnp.dot(ob_ref[...], wb_ref[...], preferred_element_type=_F32)
    acc += jax.nn.sigmoid(gc_ref[...]) * jnp.dot(oc_ref[...], wc_ref[...], preferred_element_type=_F32)
    o_ref[...] = acc.astype(o_ref.dtype)


def _gated_proj(o_a, o_b, o_c, w_a, w_b, w_c, h, gate_off, tm=1088, tn=256):
    m = o_a.shape[0]
    d = w_a.shape[1]
    tm, tn = _tile(m, tm), _tile(d, tn, LANES)
    nd, g0 = d // tn, gate_off // tn
    assert gate_off % tn == 0
    a_spec = lambda x: pl.BlockSpec((tm, x.shape[1]), lambda i, j: (i, 0))
    w_spec = lambda x: pl.BlockSpec((x.shape[0], tn), lambda i, j: (0, j))
    g_spec = lambda br: pl.BlockSpec((tm, tn), lambda i, j: (i, g0 + br * nd + j))
    return pl.pallas_call(
        _gproj_kernel, grid=(m // tm, nd),
        in_specs=[a_spec(o_a), a_spec(o_b), a_spec(o_c), w_spec(w_a), w_spec(w_b), w_spec(w_c),
                  g_spec(0), g_spec(1), g_spec(2)],
        out_specs=pl.BlockSpec((tm, tn), lambda i, j: (i, j)),
        out_shape=_SDS((m, d), _MXU), compiler_params=_params("parallel", "arbitrary"),
        name="gated_proj")(o_a, o_b, o_c, w_a, w_b, w_c, h, h, h)


def _add_ln_kernel(x_ref, r_ref, g_ref, b_ref, y_ref, yb_ref, *, alpha):
    v = alpha * x_ref[...] + r_ref[...]
    mu = jnp.mean(v, axis=-1, keepdims=True)
    c = v - mu
    var = jnp.mean(c * c, axis=-1, keepdims=True)
    y = c * lax.rsqrt(var + LN_EPS) * g_ref[...] + b_ref[...]
    y_ref[...] = y
    yb_ref[...] = y.astype(yb_ref.dtype)


def _add_ln(x, r, g, b, alpha, tm=256):
    m, d = x.shape
    tm = _tile(m, tm)
    row = pl.BlockSpec((tm, d), lambda i: (i, 0))
    vec = pl.BlockSpec((1, d), lambda i: (0, 0))
    return pl.pallas_call(
        functools.partial(_add_ln_kernel, alpha=alpha), grid=(m // tm,),
        in_specs=[row, row, vec, vec], out_specs=[row, row],
        out_shape=[_SDS((m, d), _F32), _SDS((m, d), _MXU)], compiler_params=_params("parallel"),
        name="add_ln")(x, r, g.reshape(1, d), b.reshape(1, d))


def _rows_from_tok(x, nh):
    tq, n = x.shape
    return jnp.broadcast_to(x[:, None, :], (tq, nh, n)).reshape(tq * nh, n)


def _rows_from_head(x, tq):
    nh, n = x.shape
    return jnp.broadcast_to(x[None, :, :], (tq, nh, n)).reshape(tq * nh, n)


def _expand_blocks(sel, blk):
    tq, nb = sel.shape
    lane_blk = lax.broadcasted_iota(jnp.int32, (tq, nb * blk), 1) // blk
    out = jnp.zeros((tq, nb * blk), _F32)
    for j in range(nb):
        out = jnp.where(lane_blk == j, sel[:, j:j + 1], out)
    return out


def _online_update(s, mask, v, m_ref, l_ref, acc_ref, v_transposed=False):
    m_old = m_ref[...]
    m_new = jnp.maximum(m_old, jnp.max(s, axis=-1, keepdims=True))
    p = jnp.exp(s - m_new)
    if mask is not None:
        p = jnp.where(mask, p, 0.0)
    alpha = jnp.exp(m_old - m_new)
    l_ref[...] = alpha * l_ref[...] + jnp.sum(p, axis=-1, keepdims=True)
    pv = _qk(p.astype(v.dtype), v) if v_transposed else jnp.dot(p.astype(v.dtype), v, preferred_element_type=_F32)
    acc_ref[...] = alpha * acc_ref[...] + pv
    m_ref[...] = m_new


def _qk(q, k):
    return lax.dot_general(q, k, (((1,), (1,)), ((), ())), preferred_element_type=_F32)


def _attn_dense_kernel(*refs, mode, nh, tq, ck, scale, q_pos0, k_pos0, n_valid, window, blk):
    q_ref, trow_ref, slope_ref, k_ref, v_ref = refs[:5]
    extra = refs[5:-4]
    o_ref, m_ref, l_ref, acc_ref = refs[-4:]
    q0 = pl.program_id(1) * tq
    m_ref[...] = jnp.full(m_ref.shape, _NEG, _F32)
    l_ref[...] = jnp.zeros(l_ref.shape, _F32)
    acc_ref[...] = jnp.zeros(acc_ref.shape, _F32)
    q = q_ref[...]
    post_scale = scale
    if float(math.log2(scale)).is_integer():
        q = (q.astype(_F32) * scale).astype(q.dtype)
        post_scale = None
    qpos = (q_pos0 + q0).astype(_F32) + trow_ref[...]
    first_q = q_pos0 + q0
    n_chunks = -(-n_valid // ck)
    hi = jnp.minimum((first_q + tq - 1 - k_pos0) // ck + 1, n_chunks)
    lo = 0 if window is None else jnp.maximum(first_q - window + 1 - k_pos0, 0) // ck
    n_open = jnp.clip((first_q - k_pos0 + 1) // ck, 0, n_valid // ck) if mode in ("mla", "fox") else lo

    def chunk(c, masked):
        start = pl.multiple_of(c * ck, ck)
        k = k_ref[pl.ds(start, ck), :]
        v = v_ref[pl.ds(start, ck), :]
        s = _qk(q, k)
        if post_scale is not None:
            s = s * post_scale
        if mode == "fox":
            (cb_ref,) = extra
            s = s + _rows_from_head(cb_ref[c], tq)
        if not masked:
            _online_update(s, None, v, m_ref, l_ref, acc_ref)
            return
        kidx = start + lax.broadcasted_iota(jnp.int32, (1, ck), 1)
        dist = qpos - (k_pos0 + kidx).astype(_F32)
        mask = (dist >= 0.0) & (kidx < n_valid)
        if mode in ("slc", "win"):
            s = s - slope_ref[...] * dist
            if mode == "win":
                mask = mask & (dist < float(window))
                if k_pos0 < 0:
                    mask = mask & (k_pos0 + kidx >= 0)
            else:
                (sel_ref,) = extra
                mask = mask & (_rows_from_tok(_expand_blocks(sel_ref[c], blk), nh) > 0.5)
        s = jnp.where(mask, s, _NEG)
        _online_update(s, mask, v, m_ref, l_ref, acc_ref)

    def open_body(c, carry):
        chunk(c, False)
        return carry

    def masked_body(c, carry):
        chunk(c, True)
        return carry

    if mode in ("mla", "fox"):
        lax.fori_loop(lo, n_open, open_body, 0)
    lax.fori_loop(n_open, hi, masked_body, 0)
    o_ref[...] = (acc_ref[...] / jnp.maximum(l_ref[...], 1e-30)).astype(o_ref.dtype)


def _attn_dense(mode, q, k, v, nh, *, scale, q_pos0, k_pos0=0, n_valid=None, window=None,
                slopes=None, col_bias=None, sel=None, blk=None, out_dtype=_F32, tq=32, ck=512):
    b, rows, dq = q.shape
    t = rows // nh
    s_len, dv = v.shape[1], v.shape[2]
    n_valid = s_len if n_valid is None else n_valid
    tq = _tile(t, tq, 1)
    ck = _tile(s_len, ck, LANES) if s_len % LANES == 0 else s_len
    nck = s_len // ck
    r = tq * nh
    trow = jnp.repeat(jnp.arange(tq, dtype=_F32), nh).reshape(r, 1)
    slope_rows = (jnp.tile(slopes, tq) if slopes is not None else jnp.zeros((r,), _F32)).reshape(r, 1)
    const = lambda shape: pl.BlockSpec(shape, lambda bi, i: (0,) * len(shape))
    in_specs = [pl.BlockSpec((None, r, dq), lambda bi, i: (bi, i, 0)), const((r, 1)), const((r, 1)),
                pl.BlockSpec((None, s_len, k.shape[2]), lambda bi, i: (bi, 0, 0)),
                pl.BlockSpec((None, s_len, dv), lambda bi, i: (bi, 0, 0))]
    args = [q, trow, slope_rows, k, v]
    if mode == "fox":
        cb = col_bias.reshape(b, nh, nck, ck).transpose(0, 2, 1, 3)
        in_specs += [pl.BlockSpec((None, nck, nh, ck), lambda bi, i: (bi, 0, 0, 0))]
        args += [cb]
    elif mode == "slc":
        nb = ck // blk
        sl = sel[:, :, :nck * nb].reshape(b, t, nck, nb).transpose(0, 2, 1, 3)
        in_specs += [pl.BlockSpec((None, nck, tq, nb), lambda bi, i: (bi, 0, i, 0))]
        args += [sl]
    kern = functools.partial(_attn_dense_kernel, mode=mode, nh=nh, tq=tq, ck=ck, scale=scale, q_pos0=q_pos0,
                             k_pos0=k_pos0, n_valid=n_valid, window=window, blk=blk)
    return pl.pallas_call(
        kern, grid=(b, t // tq), in_specs=in_specs,
        out_specs=pl.BlockSpec((None, r, dv), lambda bi, i: (bi, i, 0)),
        out_shape=_SDS((b, rows, dv), out_dtype),
        scratch_shapes=[pltpu.VMEM((r, 1), _F32), pltpu.VMEM((r, 1), _F32), pltpu.VMEM((r, dv), _F32)],
        compiler_params=_params("parallel", "arbitrary"), name="attn_dense_" + mode)(*args)


def _split_hi_lo(x):
    hi = x.astype(_MXU)
    lo = (x - hi.astype(_F32)).astype(_MXU)
    return hi, lo


def _attn_paged_kernel(pt_ref, *refs, mode, nh, t_new, n_pp, page, dk, dv, scale, pos0, blk):
    del pt_ref
    q_ref, trow_ref, slope_ref = refs[:3]
    page_refs = refs[3:3 + n_pp]
    new_ref = refs[3 + n_pp]
    extra = refs[4 + n_pp:-5]
    o_ref, m_ref, l_ref, acc_ref, pref_ref = refs[-5:]
    j = pl.program_id(1)
    r = nh * t_new
    feature_major = mode in ("mla", "fox")

    @pl.when(j == 0)
    def _():
        m_ref[...] = jnp.full(m_ref.shape, _NEG, _F32)
        l_ref[...] = jnp.zeros(l_ref.shape, _F32)
        acc_ref[...] = jnp.zeros(acc_ref.shape, _F32)
        pref_ref[...] = jnp.zeros(pref_ref.shape, _F32)

    q = q_ref[...]
    post_scale = scale
    if float(math.log2(scale)).is_integer():
        q = (q.astype(_F32) * scale).astype(q.dtype)
        post_scale = None
    qpos = float(pos0) + trow_ref[...]

    def cat(xs, axis):
        return xs[0] if len(xs) == 1 else jnp.concatenate(xs, axis=axis)

    def fox_cum(pages):
        row = lax.broadcasted_iota(jnp.int32, (page, page), 0)
        tri = (row <= lax.broadcasted_iota(jnp.int32, (page, page), 1)).astype(_MXU)
        ones = jnp.ones((page, page), _MXU)
        hi, lo = _split_hi_lo(cat([pg[2 * dk:2 * dk + nh, :] for pg in pages], 0))
        within = jnp.dot(hi, tri, preferred_element_type=_F32) + jnp.dot(lo, tri, preferred_element_type=_F32)
        total = jnp.dot(hi, ones, preferred_element_type=_F32) + jnp.dot(lo, ones, preferred_element_type=_F32)
        prefix = pref_ref[...]
        blocks = []
        for p in range(len(pages)):
            blocks.append(within[p * nh:(p + 1) * nh] + prefix)
            prefix = prefix + total[p * nh:(p + 1) * nh]
        pref_ref[...] = prefix
        return cat(blocks, 1)

    def process(pages, kpos0, sel_rows, is_new):
        n = len(pages) * page
        if feature_major:
            kt = cat([pg[0:dk, :].astype(_MXU) for pg in pages], 1)
            s = jnp.dot(q, kt, preferred_element_type=_F32)
            v = kt[:dv] if mode == "mla" else cat([pg[dk:dk + dv, :].astype(_MXU) for pg in pages], 1)
        else:
            s = _qk(q, cat([pg[:, 0:dk].astype(_MXU) for pg in pages], 0))
            v = cat([pg[:, dk:dk + dv].astype(_MXU) for pg in pages], 0)
        if post_scale is not None:
            s = s * post_scale
        if mode == "fox":
            s = s - _rows_from_head(fox_cum(pages), t_new)
        kidx = lax.broadcasted_iota(jnp.int32, (1, n), 1)
        mask = None
        if mode == "slc":
            dist = qpos - (kpos0 + kidx).astype(_F32)
            s = s - slope_ref[...] * dist
            mask = sel_rows > 0.5
        if is_new:
            causal = kidx.astype(_F32) <= trow_ref[...]
            mask = causal if mask is None else (mask & causal)
        if mask is not None:
            s = jnp.where(mask, s, _NEG)
        _online_update(s, mask, v, m_ref, l_ref, acc_ref, v_transposed=feature_major)

    sel_rows = None
    if mode == "slc":
        sel_ref, selnew_ref = extra
        sel_rows = _rows_from_tok(_expand_blocks(sel_ref[...], blk), nh)
    process(page_refs, j * (n_pp * page), sel_rows, False)

    @pl.when(j == pl.num_programs(1) - 1)
    def _():
        sel_new = None
        if mode == "slc":
            sel_new = jnp.broadcast_to(selnew_ref[...], (r, page))
        process([new_ref], pos0, sel_new, True)
        o_ref[...] = (acc_ref[...] / jnp.maximum(l_ref[...], 1e-30)).astype(o_ref.dtype)


def _attn_paged(mode, q, pool, layer, page_table, new_rows, nh, *, dk, dv, scale, pos0,
                slopes=None, sel=None, blk=None, out_dtype=_F32, n_pp=16):
    b, rows, dq = q.shape
    t_new = rows // nh
    feature_major = mode in ("mla", "fox")
    page = pool.shape[3] if feature_major else pool.shape[2]
    n_pages = page_table.shape[1]
    n_pp = _tile(n_pages, n_pp, 1)
    nj = n_pages // n_pp
    r = rows
    trow = jnp.repeat(jnp.arange(t_new, dtype=_F32), nh).reshape(r, 1)
    slope_rows = (jnp.tile(slopes, t_new) if slopes is not None else jnp.zeros((r,), _F32)).reshape(r, 1)
    new_page = jnp.pad(new_rows, ((0, 0), (0, page - t_new), (0, 0)))
    if feature_major:
        new_page = jnp.swapaxes(new_page, 1, 2)
        page_block = (None, None, pool.shape[2], page)
        page_index = lambda p, bi, j, pt: (layer, pt[bi, j * n_pp + p], 0, 0)
    else:
        page_block = (None, None, page, 2 * dk)
        page_index = lambda p, bi, j, pt: (layer, pt[bi, j * n_pp + p], 0, 1)
    const = lambda shape: pl.BlockSpec(shape, lambda bi, j, pt: (0,) * len(shape))
    in_specs = [pl.BlockSpec((None, r, dq), lambda bi, j, pt: (bi, 0, 0)), const((r, 1)), const((r, 1))]
    in_specs += [pl.BlockSpec(page_block, functools.partial(page_index, p)) for p in range(n_pp)]
    in_specs += [pl.BlockSpec((None,) + new_page.shape[1:], lambda bi, j, pt: (bi, 0, 0))]
    args = [q, trow, slope_rows] + [pool] * n_pp + [new_page]
    if mode == "slc":
        nb = n_pp * page // blk
        sl = sel[:, :, :nj * nb].reshape(b, t_new, nj, nb).transpose(0, 2, 1, 3)
        sel_new = jnp.repeat(sel[:, :, nj * nb], nh, axis=1).reshape(b, r, 1)
        in_specs += [pl.BlockSpec((None, None, t_new, nb), lambda bi, j, pt: (bi, j, 0, 0)),
                     pl.BlockSpec((None, r, 1), lambda bi, j, pt: (bi, 0, 0))]
        args += [sl, sel_new]
    kern = functools.partial(_attn_paged_kernel, mode=mode, nh=nh, t_new=t_new, n_pp=n_pp, page=page, dk=dk, dv=dv,
                             scale=scale, pos0=pos0, blk=blk)
    grid_spec = pltpu.PrefetchScalarGridSpec(
        num_scalar_prefetch=1, grid=(b, nj), in_specs=in_specs,
        out_specs=pl.BlockSpec((None, r, dv), lambda bi, j, pt: (bi, 0, 0)),
        scratch_shapes=[pltpu.VMEM((r, 1), _F32), pltpu.VMEM((r, 1), _F32), pltpu.VMEM((r, dv), _F32),
                        pltpu.VMEM((nh, page), _F32)])
    return pl.pallas_call(
        kern, grid_spec=grid_spec, out_shape=_SDS((b, r, dv), out_dtype),
        compiler_params=_params("parallel", "arbitrary"), name="attn_paged_" + mode)(page_table, *args)


def _compress_kernel(x_ref, pe_ref, w_ref, o_ref, xs_ref, *, blk):
    n = x_ref.shape[0] // blk
    wd = x_ref.shape[1]
    for l in range(blk):
        rows = x_ref[pl.ds(l, n, stride=blk), :]
        xs_ref[:, l * wd:(l + 1) * wd] = (rows + pe_ref[l:l + 1, :]).astype(xs_ref.dtype)
    o_ref[...] = jnp.dot(xs_ref[...], w_ref[...], preferred_element_type=_F32)


def _compress(x, pe_kv, w_kv, blk, rows_per_step, row0=0, rows=None):
    rows = x.shape[0] if rows is None else rows
    wd = pe_kv.shape[1]
    rb = _tile(rows, rows_per_step, blk * 16)
    n = rb // blk
    assert rows % blk == 0 and row0 % rb == 0
    i0 = row0 // rb
    return pl.pallas_call(
        functools.partial(_compress_kernel, blk=blk), grid=(rows // rb,),
        in_specs=[pl.BlockSpec((rb, wd), lambda i: (i0 + i, 0)), pl.BlockSpec(pe_kv.shape, lambda i: (0, 0)),
                  pl.BlockSpec(w_kv.shape, lambda i: (0, 0))],
        out_specs=pl.BlockSpec((n, wd), lambda i: (i, 0)),
        out_shape=_SDS((rows // blk, wd), _F32),
        scratch_shapes=[pltpu.VMEM((n, blk * wd), _MXU)],
        compiler_params=_params("parallel"), name="nsa_compress")(x, pe_kv, w_kv)


def _cmp_select_kernel(q_ref, trow_ref, slope_ref, kc_ref, vc_ref, o_ref, sel_ref, *,
                       nh, tq, scale, q_pos0, n_cmp, n_cols, blk, topn):
    q0 = pl.program_id(1) * tq
    ncp = kc_ref.shape[0]
    qpos = (q_pos0 + q0).astype(_F32) + trow_ref[...]
    cidx = lax.broadcasted_iota(jnp.int32, (1, ncp), 1)
    cmp_end = ((cidx + 1) * blk - 1).astype(_F32)
    dist = qpos - cmp_end
    mask = (dist >= 0.0) & (cidx < n_cmp)
    s = _qk(q_ref[...], kc_ref[...]) * scale - slope_ref[...] * dist
    s = jnp.where(mask, s, _NEG)
    m = jnp.max(s, axis=-1, keepdims=True)
    e = jnp.where(mask, jnp.exp(s - m), 0.0)
    p = e / jnp.maximum(jnp.sum(e, axis=-1, keepdims=True), 1e-30)
    o_ref[...] = jnp.dot(p.astype(vc_ref.dtype), vc_ref[...], preferred_element_type=_F32)
    imp = jnp.sum(p.reshape(tq, nh, ncp), axis=1)
    qp = (q_pos0 + q0 + lax.broadcasted_iota(jnp.int32, (tq, 1), 0)).astype(_F32)
    blk_lo = (cidx * blk).astype(_F32)
    forced = ((blk_lo <= qp) & (qp <= cmp_end)) | (cidx == 0)
    score = jnp.where(forced, SEL_FORCED, jnp.where(cmp_end <= qp, imp, SEL_INVALID))
    score = jnp.where(cidx < n_cols, score, 2.0 * SEL_INVALID)
    rank = jnp.zeros((tq, ncp), _F32)
    for j in range(n_cols):
        c = score[:, j:j + 1]
        rank = rank + jnp.where((c > score) | ((c == score) & (cidx > j)), 1.0, 0.0)
    sel_ref[...] = jnp.where((rank < float(topn)) & (score >= 0.0), 1.0, 0.0)


def _cmp_select(q, kc, vc, nh, slopes, *, scale, q_pos0, n_cmp, n_cols, blk, topn, tq=8):
    b, rows, d = q.shape
    t = rows // nh
    ncp = kc.shape[1]
    tq = _tile(t, tq, 1)
    r = tq * nh
    trow = jnp.repeat(jnp.arange(tq, dtype=_F32), nh).reshape(r, 1)
    slope_rows = jnp.tile(slopes, tq).reshape(r, 1)
    const = lambda shape: pl.BlockSpec(shape, lambda bi, i: (0,) * len(shape))
    kern = functools.partial(_cmp_select_kernel, nh=nh, tq=tq, scale=scale, q_pos0=q_pos0, n_cmp=n_cmp,
                             n_cols=n_cols, blk=blk, topn=topn)
    return pl.pallas_call(
        kern, grid=(b, t // tq),
        in_specs=[pl.BlockSpec((None, r, d), lambda bi, i: (bi, i, 0)), const((r, 1)), const((r, 1)),
                  pl.BlockSpec((None, ncp, d), lambda bi, i: (bi, 0, 0)),
                  pl.BlockSpec((None, ncp, d), lambda bi, i: (bi, 0, 0))],
        out_specs=[pl.BlockSpec((None, r, d), lambda bi, i: (bi, i, 0)),
                   pl.BlockSpec((None, tq, ncp), lambda bi, i: (bi, i, 0))],
        out_shape=[_SDS((b, rows, d), _F32), _SDS((b, t, ncp), _F32)],
        compiler_params=_params("parallel", "arbitrary"), name="nsa_cmp_select")(q, trow, slope_rows, kc, vc)


def _top_rows(x, k):
    n = x.shape[0]
    ridx = lax.broadcasted_iota(jnp.int32, x.shape, 0)
    out = []
    for _ in range(k):
        m = jnp.max(x, axis=0, keepdims=True)
        first = jnp.min(jnp.where(x == m, ridx, n), axis=0, keepdims=True)
        x = jnp.where(ridx == first, -jnp.inf, x)
        out.append(m)
    return out


def _peer_route_kernel(q_ref, keys_ref, s1_ref, s2_ref, st_ref, *, nh, half, topk):
    tm = q_ref.shape[0]
    for h in range(nh):
        q1 = q_ref[:, 2 * h * half:(2 * h + 1) * half]
        q2 = q_ref[:, (2 * h + 1) * half:(2 * h + 2) * half]
        s1 = _qk(keys_ref[h, 0], q1)
        s2 = _qk(keys_ref[h, 1], q2)
        s1_ref[h] = s1
        s2_ref[h] = s2
        v1 = _top_rows(s1, topk)
        v2 = jnp.concatenate(_top_rows(s2, topk), axis=0)
        cand = jnp.concatenate([v1[a] + v2 for a in range(topk)], axis=0)
        vals = _top_rows(cand, topk)
        z = vals[0] * 0.0
        for a in range(topk):
            z = z + jnp.exp(vals[a] - vals[0])
        st_ref[h] = jnp.concatenate([vals[topk - 1], v1[0], v2[0:1], 1.0 / z,
                                     jnp.zeros((SUBLANES - 4, tm), _F32)], axis=0)


def _peer_route(q, keys, topk, tm=256):
    m = q.shape[0]
    nh, _, nk, half = keys.shape
    tm = _tile(m, tm, LANES)
    blk3 = lambda rows: pl.BlockSpec((nh, rows, tm), lambda i: (0, 0, i))
    return pl.pallas_call(
        functools.partial(_peer_route_kernel, nh=nh, half=half, topk=topk), grid=(m // tm,),
        in_specs=[pl.BlockSpec((tm, q.shape[1]), lambda i: (i, 0)),
                  pl.BlockSpec(keys.shape, lambda i: (0, 0, 0, 0))],
        out_specs=[blk3(nk), blk3(nk), blk3(SUBLANES)],
        out_shape=[_SDS((nh, nk, m), _F32), _SDS((nh, nk, m), _F32), _SDS((nh, SUBLANES, m), _F32)],
        compiler_params=_params("parallel"), name="peer_route")(q, keys)


def _gelu(x):
    return 0.5 * x * (1.0 + lax.erf(x * (1.0 / math.sqrt(2.0))))


def _peer_hidden_kernel(x_ref, u_ref, s1_ref, s2_ref, st_ref, o_ref, e1_ref, e2_ref, *, nh, nk):
    c = pl.program_id(1)
    tc = u_ref.shape[0]
    n_i1 = tc // nk

    @pl.when(c == 0)
    def _():
        for h in range(nh):
            st = st_ref[h]
            e1_ref[h] = jnp.exp(s1_ref[h] - st[1:2]) * st[3:4]
            e2_ref[h] = jnp.exp(s2_ref[h] - st[2:3])

    parts = []
    for ii in range(n_i1):
        i1 = c * n_i1 + ii
        g = None
        for h in range(nh):
            s1row = s1_ref[h, pl.ds(i1, 1), :]
            e1row = e1_ref[h, pl.ds(i1, 1), :]
            hit = (s1row + s2_ref[h]) >= st_ref[h, 0:1, :]
            w = jnp.where(hit, e2_ref[h] * e1row, 0.0)
            g = w if g is None else g + w
        parts.append(g)
    gate_t = parts[0] if n_i1 == 1 else jnp.concatenate(parts, axis=0)
    act = _qk(x_ref[...], u_ref[...])
    o_ref[...] = (_gelu(act) * gate_t.T).astype(o_ref.dtype)


def _peer_hidden(x, u, s1, s2, st, tm=512, tc=512):
    m, d = x.shape
    n_exp = u.shape[0]
    nh, nk, _ = s1.shape
    tm = _tile(m, tm, LANES)
    tc = _tile(n_exp, tc, nk)
    tok3 = lambda rows: pl.BlockSpec((nh, rows, tm), lambda i, c: (0, 0, i))
    return pl.pallas_call(
        functools.partial(_peer_hidden_kernel, nh=nh, nk=nk), grid=(m // tm, n_exp // tc),
        in_specs=[pl.BlockSpec((tm, d), lambda i, c: (i, 0)), pl.BlockSpec((tc, d), lambda i, c: (c, 0)),
                  tok3(nk), tok3(nk), tok3(SUBLANES)],
        out_specs=pl.BlockSpec((tm, tc), lambda i, c: (i, c)),
        out_shape=_SDS((m, n_exp), _MXU),
        scratch_shapes=[pltpu.VMEM((nh, nk, tm), _F32), pltpu.VMEM((nh, nk, tm), _F32)],
        compiler_params=_params("parallel", "arbitrary"), name="peer_hidden")(x, u, s1, s2, st)


def _rmsnorm(x, g):
    return x * lax.rsqrt(jnp.mean(jnp.square(x), -1, keepdims=True) + RMS_EPS) * g


def _rope(x, pos):
    half = x.shape[-1] // 2
    inv = jnp.power(ROPE_THETA, -jnp.arange(half, dtype=_F32) / half)
    ang = pos.astype(_F32)[:, None] * inv[None, :]
    shape = (pos.shape[0],) + (1,) * (x.ndim - 2) + (half,)
    cos, sin = jnp.cos(ang).reshape(shape), jnp.sin(ang).reshape(shape)
    x1, x2 = x[..., :half], x[..., half:]
    return jnp.concatenate([x1 * cos - x2 * sin, x1 * sin + x2 * cos], -1)


def _alibi_slopes(n):
    return jnp.power(2.0, -8.0 * (jnp.arange(n, dtype=_F32) + 1.0) / n)


def _pad_rows(x, n):
    return jnp.pad(x, ((0, 0), (0, n - x.shape[1]), (0, 0)))


def kernel(x_prompt, x_sample, cache_mla, cache_fox, cache_nsa, state_nsa_win, page_table, w_in, b_f, g_q, w_uq, g_kv, w_uk, w_uv, pe_cmp_k, pe_cmp_v, w_cmp_k, w_cmp_v, w_proj_a, w_proj_b, w_proj_c, w_out, ln1_g, ln1_b, w_pq, peer_keys, peer_u, peer_v, ln2_g, ln2_b):
    bp, tp, dm = x_prompt.shape
    bs, ts, _ = x_sample.shape
    depth = w_in.shape[0]
    page = cache_mla.shape[2]
    past = page_table.shape[1] * page
    n_pool = cache_mla.shape[1]
    q_lora, mla_h, mla_qd = w_uq.shape[1:]
    kv_lora, _, nope = w_uk.shape[1:]
    rope_d = mla_qd - nope
    fox_h = b_f.shape[1]
    fox_d = (cache_fox.shape[3] - fox_h) // 2
    nsa_d = cache_nsa.shape[3] // 4
    nsa_h = w_proj_c.shape[1] // nsa_d
    blk = pe_cmp_k.shape[1]
    peer_h, dkey = w_pq.shape[2:]
    alpha = (2 * depth) ** 0.25
    mp, ms = bp * tp, bs * ts

    seg = (q_lora, kv_lora, rope_d, fox_h * fox_d, fox_d, fox_d, fox_h, nsa_h * nsa_d, 6 * nsa_d, N_BRANCH * nsa_h,
           N_BRANCH * dm)
    src = [0]
    for s_ in seg:
        src.append(src[-1] + s_)
    groups = ((0, 1), (1, 2), (2, 3), (3, 4), (4, 6), (6, 7), (7, 8), (8, 9), (9, 10))
    off, pieces, cur = {}, [], 0
    for a, z in groups:
        width = src[z] - src[a]
        off[a] = cur
        pieces.append((src[a], src[z], _round_up(width, LANES) - width))
        cur += _round_up(width, LANES)
    gate_off = _round_up(cur, 4 * LANES)
    pieces[-1] = (pieces[-1][0], pieces[-1][1], pieces[-1][2] + gate_off - cur)
    pieces.append((src[10], src[11], 0))

    def layout_w_in(w):
        cols = []
        for a, z, padw in pieces:
            cols.append(w[:, a:z].astype(_MXU))
            if padw:
                cols.append(jnp.zeros((w.shape[0], padw), _MXU))
        return jnp.concatenate(cols, axis=1)

    pos_p = jnp.arange(tp)
    pos_s = past + jnp.arange(ts)
    pos_all = jnp.concatenate([jnp.tile(pos_p, bp), jnp.tile(pos_s, bs)])
    slopes = _alibi_slopes(nsa_h)
    n_cmp_p, n_cmp_s = tp // blk, (past + ts) // blk
    n_cols_p = max(-(-tp // blk), NSA_TOPN)
    n_cols_s = max(-(-(past + ts) // blk), NSA_TOPN)
    win_buf = state_nsa_win.shape[2]

    cache_mla_t = jnp.swapaxes(cache_mla, 2, 3)
    cache_fox_t = jnp.swapaxes(cache_fox, 2, 3)
    x = jnp.concatenate([x_prompt.reshape(mp, dm), x_sample.reshape(ms, dm)], axis=0)
    xb = x.astype(_MXU)
    outs = {k: [] for k in ("mla_p", "mla_s", "fox_p", "fox_s", "nsa_p", "nsa_s", "win_p", "win_s")}

    for l in range(depth):
        h = _mm(xb, layout_w_in(w_in[l]), _F32)
        col = lambda a, width: h[:, off[a]:off[a] + width]
        mq, mkv, mkr = col(0, q_lora), col(1, kv_lora), col(2, rope_d)
        fq, fkv, ff = col(3, fox_h * fox_d), col(4, 2 * fox_d), col(6, fox_h)
        nq, nkv, ng = col(7, nsa_h * nsa_d), col(8, 6 * nsa_d), col(9, N_BRANCH * nsa_h)

        q = _mm(_rmsnorm(mq, g_q[l]).astype(_MXU), w_uq[l].reshape(q_lora, mla_h * mla_qd).astype(_MXU), _F32)
        q = q.reshape(-1, mla_h, mla_qd)
        q_lat = _headmm(q[..., :nope].reshape(-1, mla_h * nope).astype(_MXU),
                        jnp.transpose(w_uk[l], (1, 2, 0)).astype(_MXU), _MXU)
        q_eff = jnp.concatenate([q_lat.reshape(-1, mla_h, kv_lora),
                                 _rope(q[..., nope:], pos_all).astype(_MXU)], axis=-1)
        new_mla = jnp.concatenate([_rmsnorm(mkv, g_kv[l]), _rope(mkr, pos_all)], axis=-1)
        mla_scale = float(mla_qd) ** -0.5
        kp = new_mla[:mp].reshape(bp, tp, -1).astype(_MXU)
        o_p = _attn_dense("mla", q_eff[:mp].reshape(bp, tp * mla_h, -1), kp, kp[..., :kv_lora], mla_h,
                          scale=mla_scale, q_pos0=0, out_dtype=_MXU, tq=max(1, 1024 // mla_h))
        o_s = _attn_paged("mla", q_eff[mp:].reshape(bs, ts * mla_h, -1), cache_mla_t, l, page_table,
                          new_mla[mp:].reshape(bs, ts, -1), mla_h, dk=kv_lora + rope_d, dv=kv_lora,
                          scale=mla_scale, pos0=past, out_dtype=_MXU)
        o_lat = jnp.concatenate([o_p.reshape(mp, -1), o_s.reshape(ms, -1)], axis=0)
        o_a = _headmm(o_lat, jnp.transpose(w_uv[l], (1, 0, 2)).astype(_MXU), _MXU)

        logf = jax.nn.log_sigmoid(ff + b_f[l])
        new_fox = jnp.concatenate([fkv, logf], axis=-1)
        fox_scale = float(fox_d) ** -0.5
        fqb = fq.astype(_MXU)
        cum = jnp.cumsum(logf[:mp].reshape(bp, tp, fox_h), axis=1)
        o_p = _attn_dense("fox", fqb[:mp].reshape(bp, tp * fox_h, fox_d),
                          fkv[:mp, :fox_d].reshape(bp, tp, fox_d).astype(_MXU),
                          fkv[:mp, fox_d:].reshape(bp, tp, fox_d).astype(_MXU), fox_h, scale=fox_scale, q_pos0=0,
                          col_bias=-jnp.transpose(cum, (0, 2, 1)), out_dtype=_MXU)
        o_s = _attn_paged("fox", fqb[mp:].reshape(bs, ts * fox_h, fox_d), cache_fox_t, l, page_table,
                          new_fox[mp:].reshape(bs, ts, -1), fox_h, dk=fox_d, dv=fox_d,
                          scale=fox_scale, pos0=past, out_dtype=_MXU)
        o_b = jnp.concatenate([o_p.reshape(mp, -1), o_s.reshape(ms, -1)], axis=0)

        nsa_scale = float(nsa_d) ** -0.5
        nqb = nq.astype(_MXU)
        new_nsa = nkv[:, :4 * nsa_d]
        pe_kv = jnp.concatenate([pe_cmp_k[l], pe_cmp_v[l]], axis=-1)
        zero = jnp.zeros((blk, nsa_d, nsa_d), _F32)
        w_kv = jnp.concatenate([jnp.concatenate([w_cmp_k[l], zero], axis=-1),
                                jnp.concatenate([zero, w_cmp_v[l]], axis=-1)], axis=1)
        w_kv = w_kv.reshape(blk * 2 * nsa_d, 2 * nsa_d).astype(_MXU)
        cmp_p = _compress(new_nsa[:mp], pe_kv, w_kv, blk, 8192).reshape(bp, n_cmp_p, 2 * nsa_d)
        ncp_p = _round_up(n_cols_p, LANES)
        q_p = nqb[:mp].reshape(bp, tp * nsa_h, nsa_d)
        o_cmp_p, sel_p = _cmp_select(q_p, _pad_rows(cmp_p[..., :nsa_d], ncp_p).astype(_MXU),
                                     _pad_rows(cmp_p[..., nsa_d:], ncp_p).astype(_MXU), nsa_h, slopes,
                                     scale=nsa_scale, q_pos0=0, n_cmp=n_cmp_p, n_cols=n_cols_p, blk=blk, topn=NSA_TOPN)
        kv_p = nkv[:mp].reshape(bp, tp, 6 * nsa_d).astype(_MXU)
        o_slc_p = _attn_dense("slc", q_p, kv_p[..., 2 * nsa_d:3 * nsa_d], kv_p[..., 3 * nsa_d:4 * nsa_d], nsa_h,
                              scale=nsa_scale, q_pos0=0, slopes=slopes, sel=sel_p, blk=blk)
        o_win_p = _attn_dense("win", q_p, kv_p[..., 4 * nsa_d:5 * nsa_d], kv_p[..., 5 * nsa_d:], nsa_h,
                              scale=nsa_scale, q_pos0=0, slopes=slopes, window=NSA_WINDOW)
        cmp_pool = _compress(cache_nsa.reshape(depth * n_pool * page, -1), pe_kv, w_kv, blk, 64 * page,
                             row0=l * n_pool * page, rows=n_pool * page)
        cmp_s = cmp_pool.reshape(n_pool, page // blk, 2 * nsa_d)[page_table].reshape(bs, -1, 2 * nsa_d)[:, :n_cmp_s]
        ncp_s = _round_up(n_cols_s, LANES)
        q_s = nqb[mp:].reshape(bs, ts * nsa_h, nsa_d)
        o_cmp_s, sel_s = _cmp_select(q_s, _pad_rows(cmp_s[..., :nsa_d], ncp_s).astype(_MXU),
                                     _pad_rows(cmp_s[..., nsa_d:], ncp_s).astype(_MXU), nsa_h, slopes,
                                     scale=nsa_scale, q_pos0=past, n_cmp=n_cmp_s, n_cols=n_cols_s, blk=blk,
                                     topn=NSA_TOPN, tq=ts)
        o_slc_s = _attn_paged("slc", q_s, cache_nsa, l, page_table, new_nsa[mp:, 2 * nsa_d:].reshape(bs, ts, -1),
                              nsa_h, dk=nsa_d, dv=nsa_d, scale=nsa_scale, pos0=past, slopes=slopes, sel=sel_s,
                              blk=blk)
        win_rows = jnp.concatenate([state_nsa_win[l], nkv[mp:, 4 * nsa_d:].reshape(bs, ts, 2 * nsa_d)], axis=1)
        band = _pad_rows(win_rows, _round_up(win_buf + ts, LANES)).astype(_MXU)
        o_win_s = _attn_dense("win", q_s, band[..., :nsa_d], band[..., nsa_d:], nsa_h, scale=nsa_scale,
                              q_pos0=past, k_pos0=past - win_buf, n_valid=win_buf + ts, slopes=slopes,
                              window=NSA_WINDOW, tq=ts)
        gate = jax.nn.sigmoid(ng).reshape(-1, N_BRANCH, nsa_h, 1)
        br = lambda p_, s_: jnp.concatenate([p_.reshape(mp, nsa_h, nsa_d), s_.reshape(ms, nsa_h, nsa_d)], axis=0)
        o_c = (gate[:, 0] * br(o_cmp_p, o_cmp_s) + gate[:, 1] * br(o_slc_p, o_slc_s)
               + gate[:, 2] * br(o_win_p, o_win_s)).reshape(-1, nsa_h * nsa_d).astype(_MXU)

        merged = _gated_proj(o_a, o_b, o_c, w_proj_a[l].astype(_MXU), w_proj_b[l].astype(_MXU),
                             w_proj_c[l].astype(_MXU), h, gate_off)
        mix = _mm(merged, w_out[l].astype(_MXU), _F32)
        x, xb = _add_ln(x, mix, ln1_g[l], ln1_b[l], alpha)

        pq = _mm(xb, w_pq[l].reshape(dm, peer_h * dkey).astype(_MXU), _MXU)
        s1, s2, st = _peer_route(pq, peer_keys[l].astype(_MXU), PEER_TOPK)
        ffn = _mm_acc(_peer_hidden(xb, peer_u[l].astype(_MXU), s1, s2, st), peer_v[l].astype(_MXU))
        x, xb = _add_ln(x, ffn, ln2_g[l], ln2_b[l], alpha)

        keep_p = min(NSA_WINDOW, tp)
        keep_s = min(NSA_WINDOW, past + ts)
        outs["mla_p"].append(new_mla[:mp].reshape(bp, tp, -1))
        outs["mla_s"].append(new_mla[mp:].reshape(bs, ts, -1))
        outs["fox_p"].append(new_fox[:mp].reshape(bp, tp, -1))
        outs["fox_s"].append(new_fox[mp:].reshape(bs, ts, -1))
        outs["nsa_p"].append(new_nsa[:mp].reshape(bp, tp, -1))
        outs["nsa_s"].append(new_nsa[mp:].reshape(bs, ts, -1))
        outs["win_p"].append(nkv[:mp, 4 * nsa_d:].reshape(bp, tp, -1)[:, tp - keep_p:])
        outs["win_s"].append(win_rows[:, win_rows.shape[1] - keep_s:])

    return (x[:mp].reshape(bp, tp, dm), x[mp:].reshape(bs, ts, dm),
            jnp.stack(outs["mla_p"]), jnp.stack(outs["mla_s"]), jnp.stack(outs["fox_p"]), jnp.stack(outs["fox_s"]),
            jnp.stack(outs["nsa_p"]), jnp.stack(outs["nsa_s"]), jnp.stack(outs["win_p"]), jnp.stack(outs["win_s"]))
```

```python
import functools
import math

import jax
import jax.numpy as jnp
from jax import lax
from jax.experimental import pallas as pl
from jax.experimental.pallas import tpu as pltpu

ROPE_THETA = 10000.0
NSA_TOPN = 16
NSA_WINDOW = 512
SEL_FORCED = 1.0e4
SEL_INVALID = -1.0
PEER_TOPK = 16
N_BRANCH = 3
LN_EPS = 1e-5
RMS_EPS = 1e-6

LANES = 128
SUBLANES = 8
VMEM_LIMIT_BYTES = 56 * 1024 * 1024

_MXU = jnp.bfloat16
_NEG = -1e30
_F32 = jnp.float32
_SDS = jax.ShapeDtypeStruct


def _params(*sem):
    return pltpu.CompilerParams(dimension_semantics=sem, vmem_limit_bytes=VMEM_LIMIT_BYTES)


def _tile(n, target, mult=SUBLANES):
    t = min(n, target)
    t -= t % mult
    while t >= mult:
        if n % t == 0:
            return t
        t -= mult
    return n


def _round_up(n, m):
    return -(-n // m) * m


def _mm_kernel(a_ref, w_ref, o_ref):
    o_ref[...] = jnp.dot(a_ref[...], w_ref[...], preferred_element_type=_F32).astype(o_ref.dtype)


def _mm(a, w, out_dtype, tm=1088, tn=256):
    m, k = a.shape
    n = w.shape[1]
    tm, tn = _tile(m, tm), _tile(n, tn, LANES)
    return pl.pallas_call(
        _mm_kernel, grid=(m // tm, n // tn),
        in_specs=[pl.BlockSpec((tm, k), lambda i, j: (i, 0)), pl.BlockSpec((k, tn), lambda i, j: (0, j))],
        out_specs=pl.BlockSpec((tm, tn), lambda i, j: (i, j)),
        out_shape=_SDS((m, n), out_dtype), compiler_params=_params("parallel", "arbitrary"),
        name="mm")(a, w)


def _mm_acc_kernel(a_ref, w_ref, o_ref):
    upd = jnp.dot(a_ref[...], w_ref[...], preferred_element_type=_F32)

    @pl.when(pl.program_id(2) == 0)
    def _():
        o_ref[...] = upd

    @pl.when(pl.program_id(2) != 0)
    def _():
        o_ref[...] += upd


def _mm_acc(a, w, tm=1088, tn=2048, tk=1024):
    m, k = a.shape
    n = w.shape[1]
    tm, tn, tk = _tile(m, tm), _tile(n, tn, LANES), _tile(k, tk, LANES)
    return pl.pallas_call(
        _mm_acc_kernel, grid=(m // tm, n // tn, k // tk),
        in_specs=[pl.BlockSpec((tm, tk), lambda i, j, kk: (i, kk)), pl.BlockSpec((tk, tn), lambda i, j, kk: (kk, j))],
        out_specs=pl.BlockSpec((tm, tn), lambda i, j, kk: (i, j)),
        out_shape=_SDS((m, n), _F32), compiler_params=_params("parallel", "parallel", "arbitrary"),
        name="mm_acc")(a, w)


def _headmm_kernel(a_ref, w_ref, o_ref):
    o_ref[...] = jnp.dot(a_ref[...], w_ref[...], preferred_element_type=_F32).astype(o_ref.dtype)


def _headmm(a, w, out_dtype, tm=1088):
    m = a.shape[0]
    nh, din, dout = w.shape
    tm = _tile(m, tm)
    return pl.pallas_call(
        _headmm_kernel, grid=(m // tm, nh),
        in_specs=[pl.BlockSpec((tm, din), lambda i, h: (i, h)), pl.BlockSpec((None, din, dout), lambda i, h: (h, 0, 0))],
        out_specs=pl.BlockSpec((tm, dout), lambda i, h: (i, h)),
        out_shape=_SDS((m, nh * dout), out_dtype), compiler_params=_params("parallel", "arbitrary"),
        name="headmm")(a, w)


def _gproj_kernel(oa_ref, ob_ref, oc_ref, wa_ref, wb_ref, wc_ref, ga_ref, gb_ref, gc_ref, o_ref):
    acc = jax.nn.sigmoid(ga_ref[...]) * jnp.dot(oa_ref[...], wa_ref[...], preferred_element_type=_F32)
    acc += jax.nn.sigmoid(gb_ref[...]) * jnp.dot(ob_ref[...], wb_ref[...], preferred_element_type=_F32)
    acc += jax.nn.sigmoid(gc_ref[...]) * jnp.dot(oc_ref[...], wc_ref[...], preferred_element_type=_F32)
    o_ref[...] = acc.astype(o_ref.dtype)


def _gated_proj(o_a, o_b, o_c, w_a, w_b, w_c, h, gate_off, tm=1088, tn=256):
    m = o_a.shape[0]
    d = w_a.shape[1]
    tm, tn = _tile(m, tm), _tile(d, tn, LANES)
    nd, g0 = d // tn, gate_off // tn
    assert gate_off % tn == 0
    a_spec = lambda x: pl.BlockSpec((tm, x.shape[1]), lambda i, j: (i, 0))
    w_spec = lambda x: pl.BlockSpec((x.shape[0], tn), lambda i, j: (0, j))
    g_spec = lambda br: pl.BlockSpec((tm, tn), lambda i, j: (i, g0 + br * nd + j))
    return pl.pallas_call(
        _gproj_kernel, grid=(m // tm, nd),
        in_specs=[a_spec(o_a), a_spec(o_b), a_spec(o_c), w_spec(w_a), w_spec(w_b), w_spec(w_c),
                  g_spec(0), g_spec(1), g_spec(2)],
        out_specs=pl.BlockSpec((tm, tn), lambda i, j: (i, j)),
        out_shape=_SDS((m, d), _MXU), compiler_params=_params("parallel", "arbitrary"),
        name="gated_proj")(o_a, o_b, o_c, w_a, w_b, w_c, h, h, h)


def _add_ln_kernel(x_ref, r_ref, g_ref, b_ref, y_ref, yb_ref, *, alpha):
    v = alpha * x_ref[...] + r_ref[...]
    mu = jnp.mean(v, axis=-1, keepdims=True)
    c = v - mu
    var = jnp.mean(c * c, axis=-1, keepdims=True)
    y = c * lax.rsqrt(var + LN_EPS) * g_ref[...] + b_ref[...]
    y_ref[...] = y
    yb_ref[...] = y.astype(yb_ref.dtype)


def _add_ln(x, r, g, b, alpha, tm=256):
    m, d = x.shape
    tm = _tile(m, tm)
    row = pl.BlockSpec((tm, d), lambda i: (i, 0))
    vec = pl.BlockSpec((1, d), lambda i: (0, 0))
    return pl.pallas_call(
        functools.partial(_add_ln_kernel, alpha=alpha), grid=(m // tm,),
        in_specs=[row, row, vec, vec], out_specs=[row, row],
        out_shape=[_SDS((m, d), _F32), _SDS((m, d), _MXU)], compiler_params=_params("parallel"),
        name="add_ln")(x, r, g.reshape(1, d), b.reshape(1, d))


def _rows_from_tok(x, nh):
    tq, n = x.shape
    return jnp.broadcast_to(x[:, None, :], (tq, nh, n)).reshape(tq * nh, n)


def _rows_from_head(x, tq):
    nh, n = x.shape
    return jnp.broadcast_to(x[None, :, :], (tq, nh, n)).reshape(tq * nh, n)


def _expand_blocks(sel, blk):
    tq, nb = sel.shape
    lane_blk = lax.broadcasted_iota(jnp.int32, (tq, nb * blk), 1) // blk
    out = jnp.zeros((tq, nb * blk), _F32)
    for j in range(nb):
        out = jnp.where(lane_blk == j, sel[:, j:j + 1], out)
    return out


def _online_update(s, mask, v, m_ref, l_ref, acc_ref, v_transposed=False):
    m_old = m_ref[...]
    m_new = jnp.maximum(m_old, jnp.max(s, axis=-1, keepdims=True))
    p = jnp.exp(s - m_new)
    if mask is not None:
        p = jnp.where(mask, p, 0.0)
    alpha = jnp.exp(m_old - m_new)
    l_ref[...] = alpha * l_ref[...] + jnp.sum(p, axis=-1, keepdims=True)
    pv = _qk(p.astype(v.dtype), v) if v_transposed else jnp.dot(p.astype(v.dtype), v, preferred_element_type=_F32)
    acc_ref[...] = alpha * acc_ref[...] + pv
    m_ref[...] = m_new


def _qk(q, k):
    return lax.dot_general(q, k, (((1,), (1,)), ((), ())), preferred_element_type=_F32)


def _attn_dense_kernel(*refs, mode, nh, tq, ck, rsub, scale, q_pos0, k_pos0, n_valid, window, blk):
    q_ref, trow_ref, slope_ref, k_ref, v_ref = refs[:5]
    extra = refs[5:-4]
    o_ref, m_ref, l_ref, acc_ref = refs[-4:]
    q0 = pl.program_id(1) * tq
    m_ref[...] = jnp.full(m_ref.shape, _NEG, _F32)
    l_ref[...] = jnp.zeros(l_ref.shape, _F32)
    acc_ref[...] = jnp.zeros(acc_ref.shape, _F32)
    q = q_ref[...]
    post_scale = scale
    if float(math.log2(scale)).is_integer():
        q = (q.astype(_F32) * scale).astype(q.dtype)
        post_scale = None
    qpos = (q_pos0 + q0).astype(_F32) + trow_ref[...]
    first_q = q_pos0 + q0
    n_chunks = -(-n_valid // ck)
    hi = jnp.minimum((first_q + tq - 1 - k_pos0) // ck + 1, n_chunks)
    lo = 0 if window is None else jnp.maximum(first_q - window + 1 - k_pos0, 0) // ck
    n_open = jnp.clip((first_q - k_pos0 + 1) // ck, 0, n_valid // ck) if mode in ("mla", "fox") else lo

    def chunk(c, masked):
        start = pl.multiple_of(c * ck, ck)
        k = k_ref[pl.ds(start, ck), :]
        v = v_ref[pl.ds(start, ck), :]
        kidx = start + lax.broadcasted_iota(jnp.int32, (1, ck), 1)
        kpos = (k_pos0 + kidx).astype(_F32)
        for r0 in range(0, tq * nh, rsub):
            rows = slice(r0, r0 + rsub)
            t0, nt = r0 // nh, rsub // nh
            s = _qk(q[rows], k)
            if post_scale is not None:
                s = s * post_scale
            if mode == "fox":
                (cb_ref,) = extra
                s = s + _rows_from_head(cb_ref[c], nt)
            stats = (m_ref.at[rows], l_ref.at[rows], acc_ref.at[rows])
            if not masked:
                _online_update(s, None, v, *stats)
                continue
            dist = qpos[rows] - kpos
            mask = (dist >= 0.0) & (kidx < n_valid)
            if mode in ("slc", "win"):
                s = s - slope_ref[rows, :] * dist
                if mode == "win":
                    mask = mask & (dist < float(window))
                    if k_pos0 < 0:
                        mask = mask & (k_pos0 + kidx >= 0)
                else:
                    (sel_ref,) = extra
                    mask = mask & (_rows_from_tok(_expand_blocks(sel_ref[c, t0:t0 + nt, :], blk), nh) > 0.5)
            s = jnp.where(mask, s, _NEG)
            _online_update(s, mask, v, *stats)

    def open_body(c, carry):
        chunk(c, False)
        return carry

    def masked_body(c, carry):
        chunk(c, True)
        return carry

    if mode in ("mla", "fox"):
        lax.fori_loop(lo, n_open, open_body, 0)
    lax.fori_loop(n_open, hi, masked_body, 0)
    o_ref[...] = (acc_ref[...] / jnp.maximum(l_ref[...], 1e-30)).astype(o_ref.dtype)


def _attn_dense(mode, q, k, v, nh, *, scale, q_pos0, k_pos0=0, n_valid=None, window=None,
                slopes=None, col_bias=None, sel=None, blk=None, out_dtype=_F32, tq=32, ck=512, rsub=256):
    b, rows, dq = q.shape
    t = rows // nh
    s_len, dv = v.shape[1], v.shape[2]
    n_valid = s_len if n_valid is None else n_valid
    tq = _tile(t, tq, 1)
    ck = _tile(s_len, ck, LANES) if s_len % LANES == 0 else s_len
    nck = s_len // ck
    r = tq * nh
    trow = jnp.repeat(jnp.arange(tq, dtype=_F32), nh).reshape(r, 1)
    slope_rows = (jnp.tile(slopes, tq) if slopes is not None else jnp.zeros((r,), _F32)).reshape(r, 1)
    const = lambda shape: pl.BlockSpec(shape, lambda bi, i: (0,) * len(shape))
    in_specs = [pl.BlockSpec((None, r, dq), lambda bi, i: (bi, i, 0)), const((r, 1)), const((r, 1)),
                pl.BlockSpec((None, s_len, k.shape[2]), lambda bi, i: (bi, 0, 0)),
                pl.BlockSpec((None, s_len, dv), lambda bi, i: (bi, 0, 0))]
    args = [q, trow, slope_rows, k, v]
    if mode == "fox":
        cb = col_bias.reshape(b, nh, nck, ck).transpose(0, 2, 1, 3)
        in_specs += [pl.BlockSpec((None, nck, nh, ck), lambda bi, i: (bi, 0, 0, 0))]
        args += [cb]
    elif mode == "slc":
        nb = ck // blk
        sl = sel[:, :, :nck * nb].reshape(b, t, nck, nb).transpose(0, 2, 1, 3)
        in_specs += [pl.BlockSpec((None, nck, tq, nb), lambda bi, i: (bi, 0, i, 0))]
        args += [sl]
    rsub = nh * _tile(tq, max(1, rsub // nh), 1)
    kern = functools.partial(_attn_dense_kernel, mode=mode, nh=nh, tq=tq, ck=ck, rsub=rsub, scale=scale, q_pos0=q_pos0,
                             k_pos0=k_pos0, n_valid=n_valid, window=window, blk=blk)
    return pl.pallas_call(
        kern, grid=(b, t // tq), in_specs=in_specs,
        out_specs=pl.BlockSpec((None, r, dv), lambda bi, i: (bi, i, 0)),
        out_shape=_SDS((b, rows, dv), out_dtype),
        scratch_shapes=[pltpu.VMEM((r, 1), _F32), pltpu.VMEM((r, 1), _F32), pltpu.VMEM((r, dv), _F32)],
        compiler_params=_params("parallel", "arbitrary"), name="attn_dense_" + mode)(*args)


def _split_hi_lo(x):
    hi = x.astype(_MXU)
    lo = (x - hi.astype(_F32)).astype(_MXU)
    return hi, lo


def _attn_paged_kernel(pt_ref, *refs, mode, nh, t_new, n_seq, n_pp, page, dk, dv, scale, pos0, blk):
    del pt_ref
    q_ref, trow_ref, slope_ref = refs[:3]
    page_refs = refs[3:3 + n_seq * n_pp]
    new_ref = refs[3 + n_seq * n_pp]
    extra = refs[4 + n_seq * n_pp:-5]
    o_ref, m_ref, l_ref, acc_ref, pref_ref = refs[-5:]
    j = pl.program_id(1)
    r = nh * t_new
    feature_major = mode in ("mla", "fox")

    @pl.when(j == 0)
    def _():
        m_ref[...] = jnp.full(m_ref.shape, _NEG, _F32)
        l_ref[...] = jnp.zeros(l_ref.shape, _F32)
        acc_ref[...] = jnp.zeros(acc_ref.shape, _F32)
        pref_ref[...] = jnp.zeros(pref_ref.shape, _F32)

    post_scale = None if float(math.log2(scale)).is_integer() else scale
    qpos = float(pos0) + trow_ref[...]

    def cat(xs, axis):
        return xs[0] if len(xs) == 1 else jnp.concatenate(xs, axis=axis)

    def fox_cum(g, pages):
        row = lax.broadcasted_iota(jnp.int32, (page, page), 0)
        tri = (row <= lax.broadcasted_iota(jnp.int32, (page, page), 1)).astype(_MXU)
        ones = jnp.ones((page, page), _MXU)
        hi, lo = _split_hi_lo(cat([pg[2 * dk:2 * dk + nh, :] for pg in pages], 0))
        within = jnp.dot(hi, tri, preferred_element_type=_F32) + jnp.dot(lo, tri, preferred_element_type=_F32)
        total = jnp.dot(hi, ones, preferred_element_type=_F32) + jnp.dot(lo, ones, preferred_element_type=_F32)
        prefix = pref_ref[g]
        blocks = []
        for p in range(len(pages)):
            blocks.append(within[p * nh:(p + 1) * nh] + prefix)
            prefix = prefix + total[p * nh:(p + 1) * nh]
        pref_ref[g] = prefix
        return cat(blocks, 1)

    def process(g, pages, kpos0, sel_rows, is_new):
        n = len(pages) * page
        q = q_ref[g]
        if post_scale is None:
            q = (q.astype(_F32) * scale).astype(q.dtype)
        if feature_major:
            kt = cat([pg[0:dk, :].astype(_MXU) for pg in pages], 1)
            s = jnp.dot(q, kt, preferred_element_type=_F32)
            v = kt[:dv] if mode == "mla" else cat([pg[dk:dk + dv, :].astype(_MXU) for pg in pages], 1)
        else:
            s = _qk(q, cat([pg[:, 0:dk].astype(_MXU) for pg in pages], 0))
            v = cat([pg[:, dk:dk + dv].astype(_MXU) for pg in pages], 0)
        if post_scale is not None:
            s = s * post_scale
        if mode == "fox":
            s = s - _rows_from_head(fox_cum(g, pages), t_new)
        kidx = lax.broadcasted_iota(jnp.int32, (1, n), 1)
        mask = None
        if mode == "slc":
            dist = qpos - (kpos0 + kidx).astype(_F32)
            s = s - slope_ref[...] * dist
            mask = sel_rows > 0.5
        if is_new:
            causal = kidx.astype(_F32) <= trow_ref[...]
            mask = causal if mask is None else (mask & causal)
        if mask is not None:
            s = jnp.where(mask, s, _NEG)
        _online_update(s, mask, v, m_ref.at[g], l_ref.at[g], acc_ref.at[g], v_transposed=feature_major)

    for g in range(n_seq):
        sel_rows = None
        if mode == "slc":
            sel_rows = _rows_from_tok(_expand_blocks(extra[0][g], blk), nh)
        process(g, page_refs[g * n_pp:(g + 1) * n_pp], j * (n_pp * page), sel_rows, False)

    @pl.when(j == pl.num_programs(1) - 1)
    def _():
        for g in range(n_seq):
            sel_new = None
            if mode == "slc":
                sel_new = jnp.broadcast_to(extra[1][g], (r, page))
            process(g, [new_ref.at[g]], pos0, sel_new, True)
            o_ref[g] = (acc_ref[g] / jnp.maximum(l_ref[g], 1e-30)).astype(o_ref.dtype)


def _attn_paged(mode, q, pool, layer, page_table, new_rows, nh, *, dk, dv, scale, pos0,
                slopes=None, sel=None, blk=None, out_dtype=_F32, n_pp=16, n_seq=2):
    b, rows, dq = q.shape
    t_new = rows // nh
    feature_major = mode in ("mla", "fox")
    page = pool.shape[3] if feature_major else pool.shape[2]
    n_pages = page_table.shape[1]
    n_pp = _tile(n_pages, n_pp, 1)
    n_seq = _tile(b, n_seq, 1)
    nj = n_pages // n_pp
    r = rows
    trow = jnp.repeat(jnp.arange(t_new, dtype=_F32), nh).reshape(r, 1)
    slope_rows = (jnp.tile(slopes, t_new) if slopes is not None else jnp.zeros((r,), _F32)).reshape(r, 1)
    new_page = jnp.pad(new_rows, ((0, 0), (0, page - t_new), (0, 0)))
    if feature_major:
        new_page = jnp.swapaxes(new_page, 1, 2)
        page_block = (None, None, pool.shape[2], page)
        page_index = lambda g, p, bi, j, pt: (layer, pt[bi * n_seq + g, j * n_pp + p], 0, 0)
    else:
        page_block = (None, None, page, 2 * dk)
        page_index = lambda g, p, bi, j, pt: (layer, pt[bi * n_seq + g, j * n_pp + p], 0, 1)
    const = lambda shape: pl.BlockSpec(shape, lambda bi, j, pt: (0,) * len(shape))
    in_specs = [pl.BlockSpec((n_seq, r, dq), lambda bi, j, pt: (bi, 0, 0)), const((r, 1)), const((r, 1))]
    in_specs += [pl.BlockSpec(page_block, functools.partial(page_index, g, p))
                 for g in range(n_seq) for p in range(n_pp)]
    in_specs += [pl.BlockSpec((n_seq,) + new_page.shape[1:], lambda bi, j, pt: (bi, 0, 0))]
    args = [q, trow, slope_rows] + [pool] * (n_seq * n_pp) + [new_page]
    if mode == "slc":
        nb = n_pp * page // blk
        sl = sel[:, :, :nj * nb].reshape(b, t_new, nj, nb).transpose(0, 2, 1, 3)
        sel_new = jnp.repeat(sel[:, :, nj * nb], nh, axis=1).reshape(b, r, 1)
        in_specs += [pl.BlockSpec((n_seq, None, t_new, nb), lambda bi, j, pt: (bi, j, 0, 0)),
                     pl.BlockSpec((n_seq, r, 1), lambda bi, j, pt: (bi, 0, 0))]
        args += [sl, sel_new]
    kern = functools.partial(_attn_paged_kernel, mode=mode, nh=nh, t_new=t_new, n_seq=n_seq, n_pp=n_pp, page=page,
                             dk=dk, dv=dv, scale=scale, pos0=pos0, blk=blk)
    grid_spec = pltpu.PrefetchScalarGridSpec(
        num_scalar_prefetch=1, grid=(b // n_seq, nj), in_specs=in_specs,
        out_specs=pl.BlockSpec((n_seq, r, dv), lambda bi, j, pt: (bi, 0, 0)),
        scratch_shapes=[pltpu.VMEM((n_seq, r, 1), _F32), pltpu.VMEM((n_seq, r, 1), _F32),
                        pltpu.VMEM((n_seq, r, dv), _F32), pltpu.VMEM((n_seq, nh, page), _F32)])
    return pl.pallas_call(
        kern, grid_spec=grid_spec, out_shape=_SDS((b, r, dv), out_dtype),
        compiler_params=_params("parallel", "arbitrary"), name="attn_paged_" + mode)(page_table, *args)


def _compress_kernel(x_ref, pe_ref, w_ref, o_ref, xs_ref, *, blk):
    n = x_ref.shape[0] // blk
    wd = x_ref.shape[1]
    for l in range(blk):
        rows = x_ref[pl.ds(l, n, stride=blk), :]
        xs_ref[:, l * wd:(l + 1) * wd] = (rows + pe_ref[l:l + 1, :]).astype(xs_ref.dtype)
    o_ref[...] = jnp.dot(xs_ref[...], w_ref[...], preferred_element_type=_F32)


def _compress(x, pe_kv, w_kv, blk, rows_per_step, row0=0, rows=None):
    rows = x.shape[0] if rows is None else rows
    wd = pe_kv.shape[1]
    rb = _tile(rows, rows_per_step, blk * 16)
    n = rb // blk
    assert rows % blk == 0 and row0 % rb == 0
    i0 = row0 // rb
    return pl.pallas_call(
        functools.partial(_compress_kernel, blk=blk), grid=(rows // rb,),
        in_specs=[pl.BlockSpec((rb, wd), lambda i: (i0 + i, 0)), pl.BlockSpec(pe_kv.shape, lambda i: (0, 0)),
                  pl.BlockSpec(w_kv.shape, lambda i: (0, 0))],
        out_specs=pl.BlockSpec((n, wd), lambda i: (i, 0)),
        out_shape=_SDS((rows // blk, wd), _F32),
        scratch_shapes=[pltpu.VMEM((n, blk * wd), _MXU)],
        compiler_params=_params("parallel"), name="nsa_compress")(x, pe_kv, w_kv)


def _cmp_select_kernel(q_ref, trow_ref, slope_ref, kc_ref, vc_ref, o_ref, sel_ref, *,
                       nh, tq, scale, q_pos0, n_cmp, n_cols, blk, topn):
    q0 = pl.program_id(1) * tq
    ncp = kc_ref.shape[0]
    qpos = (q_pos0 + q0).astype(_F32) + trow_ref[...]
    cidx = lax.broadcasted_iota(jnp.int32, (1, ncp), 1)
    cmp_end = ((cidx + 1) * blk - 1).astype(_F32)
    dist = qpos - cmp_end
    mask = (dist >= 0.0) & (cidx < n_cmp)
    s = _qk(q_ref[...], kc_ref[...]) * scale - slope_ref[...] * dist
    s = jnp.where(mask, s, _NEG)
    m = jnp.max(s, axis=-1, keepdims=True)
    e = jnp.where(mask, jnp.exp(s - m), 0.0)
    p = e / jnp.maximum(jnp.sum(e, axis=-1, keepdims=True), 1e-30)
    o_ref[...] = jnp.dot(p.astype(vc_ref.dtype), vc_ref[...], preferred_element_type=_F32)
    imp = jnp.sum(p.reshape(tq, nh, ncp), axis=1)
    qp = (q_pos0 + q0 + lax.broadcasted_iota(jnp.int32, (tq, 1), 0)).astype(_F32)
    blk_lo = (cidx * blk).astype(_F32)
    forced = ((blk_lo <= qp) & (qp <= cmp_end)) | (cidx == 0)
    score = jnp.where(forced, SEL_FORCED, jnp.where(cmp_end <= qp, imp, SEL_INVALID))
    score = jnp.where(cidx < n_cols, score, 2.0 * SEL_INVALID)
    rank = jnp.zeros((tq, ncp), _F32)
    for j in range(n_cols):
        c = score[:, j:j + 1]
        rank = rank + jnp.where((c > score) | ((c == score) & (cidx > j)), 1.0, 0.0)
    sel_ref[...] = jnp.where((rank < float(topn)) & (score >= 0.0), 1.0, 0.0)


def _cmp_select(q, kc, vc, nh, slopes, *, scale, q_pos0, n_cmp, n_cols, blk, topn, tq=8):
    b, rows, d = q.shape
    t = rows // nh
    ncp = kc.shape[1]
    tq = _tile(t, tq, 1)
    r = tq * nh
    trow = jnp.repeat(jnp.arange(tq, dtype=_F32), nh).reshape(r, 1)
    slope_rows = jnp.tile(slopes, tq).reshape(r, 1)
    const = lambda shape: pl.BlockSpec(shape, lambda bi, i: (0,) * len(shape))
    kern = functools.partial(_cmp_select_kernel, nh=nh, tq=tq, scale=scale, q_pos0=q_pos0, n_cmp=n_cmp,
                             n_cols=n_cols, blk=blk, topn=topn)
    return pl.pallas_call(
        kern, grid=(b, t // tq),
        in_specs=[pl.BlockSpec((None, r, d), lambda bi, i: (bi, i, 0)), const((r, 1)), const((r, 1)),
                  pl.BlockSpec((None, ncp, d), lambda bi, i: (bi, 0, 0)),
                  pl.BlockSpec((None, ncp, d), lambda bi, i: (bi, 0, 0))],
        out_specs=[pl.BlockSpec((None, r, d), lambda bi, i: (bi, i, 0)),
                   pl.BlockSpec((None, tq, ncp), lambda bi, i: (bi, i, 0))],
        out_shape=[_SDS((b, rows, d), _F32), _SDS((b, t, ncp), _F32)],
        compiler_params=_params("parallel", "arbitrary"), name="nsa_cmp_select")(q, trow, slope_rows, kc, vc)


def _top_rows(x, k):
    n = x.shape[0]
    ridx = lax.broadcasted_iota(jnp.int32, x.shape, 0)
    out = []
    for _ in range(k):
        m = jnp.max(x, axis=0, keepdims=True)
        first = jnp.min(jnp.where(x == m, ridx, n), axis=0, keepdims=True)
        x = jnp.where(ridx == first, -jnp.inf, x)
        out.append(m)
    return out


def _peer_route_kernel(q_ref, keys_ref, s1_ref, s2_ref, st_ref, *, nh, half, topk):
    tm = q_ref.shape[0]
    for h in range(nh):
        q1 = q_ref[:, 2 * h * half:(2 * h + 1) * half]
        q2 = q_ref[:, (2 * h + 1) * half:(2 * h + 2) * half]
        s1 = _qk(keys_ref[h, 0], q1)
        s2 = _qk(keys_ref[h, 1], q2)
        s1_ref[h] = s1
        s2_ref[h] = s2
        v1 = _top_rows(s1, topk)
        v2 = _top_rows(s2, topk)
        pairs = [v1[a] + v2[b] for a in range(topk) for b in range(topk // (a + 1))]
        pad = _round_up(len(pairs), SUBLANES) - len(pairs)
        cand = jnp.concatenate(pairs + [jnp.full((pad, tm), -jnp.inf, _F32)], axis=0)
        vals = _top_rows(cand, topk)
        z = vals[0] * 0.0
        for a in range(topk):
            z = z + jnp.exp(vals[a] - vals[0])
        st_ref[h] = jnp.concatenate([vals[topk - 1], v1[0], v2[0], 1.0 / z,
                                     jnp.zeros((SUBLANES - 4, tm), _F32)], axis=0)


def _peer_route(q, keys, topk, tm=256):
    m = q.shape[0]
    nh, _, nk, half = keys.shape
    tm = _tile(m, tm, LANES)
    blk3 = lambda rows: pl.BlockSpec((nh, rows, tm), lambda i: (0, 0, i))
    return pl.pallas_call(
        functools.partial(_peer_route_kernel, nh=nh, half=half, topk=topk), grid=(m // tm,),
        in_specs=[pl.BlockSpec((tm, q.shape[1]), lambda i: (i, 0)),
                  pl.BlockSpec(keys.shape, lambda i: (0, 0, 0, 0))],
        out_specs=[blk3(nk), blk3(nk), blk3(SUBLANES)],
        out_shape=[_SDS((nh, nk, m), _F32), _SDS((nh, nk, m), _F32), _SDS((nh, SUBLANES, m), _F32)],
        compiler_params=_params("parallel"), name="peer_route")(q, keys)


def _gelu(x):
    return 0.5 * x * (1.0 + lax.erf(x * (1.0 / math.sqrt(2.0))))


def _peer_hidden_kernel(x_ref, u_ref, s1_ref, s2_ref, st_ref, o_ref, e1_ref, e2_ref, *, nh, nk):
    c = pl.program_id(1)
    tc = u_ref.shape[0]
    n_i1 = tc // nk

    @pl.when(c == 0)
    def _():
        for h in range(nh):
            st = st_ref[h]
            e1_ref[h] = jnp.exp(s1_ref[h] - st[1:2]) * st[3:4]
            e2_ref[h] = jnp.exp(s2_ref[h] - st[2:3])

    parts = []
    for ii in range(n_i1):
        i1 = c * n_i1 + ii
        g = None
        for h in range(nh):
            s1row = s1_ref[h, pl.ds(i1, 1), :]
            e1row = e1_ref[h, pl.ds(i1, 1), :]
            hit = (s1row + s2_ref[h]) >= st_ref[h, 0:1, :]
            w = jnp.where(hit, e2_ref[h] * e1row, 0.0)
            g = w if g is None else g + w
        parts.append(g)
    gate_t = parts[0] if n_i1 == 1 else jnp.concatenate(parts, axis=0)
    act = _qk(x_ref[...], u_ref[...])
    o_ref[...] = (_gelu(act) * gate_t.T).astype(o_ref.dtype)


def _peer_hidden(x, u, s1, s2, st, tm=512, tc=512):
    m, d = x.shape
    n_exp = u.shape[0]
    nh, nk, _ = s1.shape
    tm = _tile(m, tm, LANES)
    tc = _tile(n_exp, tc, nk)
    tok3 = lambda rows: pl.BlockSpec((nh, rows, tm), lambda i, c: (0, 0, i))
    return pl.pallas_call(
        functools.partial(_peer_hidden_kernel, nh=nh, nk=nk), grid=(m // tm, n_exp // tc),
        in_specs=[pl.BlockSpec((tm, d), lambda i, c: (i, 0)), pl.BlockSpec((tc, d), lambda i, c: (c, 0)),
                  tok3(nk), tok3(nk), tok3(SUBLANES)],
        out_specs=pl.BlockSpec((tm, tc), lambda i, c: (i, c)),
        out_shape=_SDS((m, n_exp), _MXU),
        scratch_shapes=[pltpu.VMEM((nh, nk, tm), _F32), pltpu.VMEM((nh, nk, tm), _F32)],
        compiler_params=_params("parallel", "arbitrary"), name="peer_hidden")(x, u, s1, s2, st)


def _rmsnorm(x, g):
    return x * lax.rsqrt(jnp.mean(jnp.square(x), -1, keepdims=True) + RMS_EPS) * g


def _rope(x, pos):
    half = x.shape[-1] // 2
    inv = jnp.power(ROPE_THETA, -jnp.arange(half, dtype=_F32) / half)
    ang = pos.astype(_F32)[:, None] * inv[None, :]
    shape = (pos.shape[0],) + (1,) * (x.ndim - 2) + (half,)
    cos, sin = jnp.cos(ang).reshape(shape), jnp.sin(ang).reshape(shape)
    x1, x2 = x[..., :half], x[..., half:]
    return jnp.concatenate([x1 * cos - x2 * sin, x1 * sin + x2 * cos], -1)


def _alibi_slopes(n):
    return jnp.power(2.0, -8.0 * (jnp.arange(n, dtype=_F32) + 1.0) / n)


def _pad_rows(x, n):
    return jnp.pad(x, ((0, 0), (0, n - x.shape[1]), (0, 0)))


def kernel(x_prompt, x_sample, cache_mla, cache_fox, cache_nsa, state_nsa_win, page_table, w_in, b_f, g_q, w_uq, g_kv, w_uk, w_uv, pe_cmp_k, pe_cmp_v, w_cmp_k, w_cmp_v, w_proj_a, w_proj_b, w_proj_c, w_out, ln1_g, ln1_b, w_pq, peer_keys, peer_u, peer_v, ln2_g, ln2_b):
    bp, tp, dm = x_prompt.shape
    bs, ts, _ = x_sample.shape
    depth = w_in.shape[0]
    page = cache_mla.shape[2]
    past = page_table.shape[1] * page
    n_pool = cache_mla.shape[1]
    q_lora, mla_h, mla_qd = w_uq.shape[1:]
    kv_lora, _, nope = w_uk.shape[1:]
    rope_d = mla_qd - nope
    fox_h = b_f.shape[1]
    fox_d = (cache_fox.shape[3] - fox_h) // 2
    nsa_d = cache_nsa.shape[3] // 4
    nsa_h = w_proj_c.shape[1] // nsa_d
    blk = pe_cmp_k.shape[1]
    peer_h, dkey = w_pq.shape[2:]
    alpha = (2 * depth) ** 0.25
    mp, ms = bp * tp, bs * ts

    seg = (q_lora, kv_lora, rope_d, fox_h * fox_d, fox_d, fox_d, fox_h, nsa_h * nsa_d, 6 * nsa_d, N_BRANCH * nsa_h,
           N_BRANCH * dm)
    src = [0]
    for s_ in seg:
        src.append(src[-1] + s_)
    groups = ((0, 1), (1, 2), (2, 3), (3, 4), (4, 6), (6, 7), (7, 8), (8, 9), (9, 10))
    off, pieces, cur = {}, [], 0
    for a, z in groups:
        width = src[z] - src[a]
        off[a] = cur
        pieces.append((src[a], src[z], _round_up(width, LANES) - width))
        cur += _round_up(width, LANES)
    gate_off = _round_up(cur, 4 * LANES)
    pieces[-1] = (pieces[-1][0], pieces[-1][1], pieces[-1][2] + gate_off - cur)
    pieces.append((src[10], src[11], 0))

    def layout_w_in(w):
        cols = []
        for a, z, padw in pieces:
            cols.append(w[:, a:z].astype(_MXU))
            if padw:
                cols.append(jnp.zeros((w.shape[0], padw), _MXU))
        return jnp.concatenate(cols, axis=1)

    pos_p = jnp.arange(tp)
    pos_s = past + jnp.arange(ts)
    pos_all = jnp.concatenate([jnp.tile(pos_p, bp), jnp.tile(pos_s, bs)])
    slopes = _alibi_slopes(nsa_h)
    n_cmp_p, n_cmp_s = tp // blk, (past + ts) // blk
    n_cols_p = max(-(-tp // blk), NSA_TOPN)
    n_cols_s = max(-(-(past + ts) // blk), NSA_TOPN)
    win_buf = state_nsa_win.shape[2]

    cache_mla_t = jnp.swapaxes(cache_mla, 2, 3)
    cache_fox_t = jnp.swapaxes(cache_fox, 2, 3)
    x = jnp.concatenate([x_prompt.reshape(mp, dm), x_sample.reshape(ms, dm)], axis=0)
    xb = x.astype(_MXU)
    outs = {k: [] for k in ("mla_p", "mla_s", "fox_p", "fox_s", "nsa_p", "nsa_s", "win_p", "win_s")}

    for l in range(depth):
        h = _mm(xb, layout_w_in(w_in[l]), _F32)
        col = lambda a, width: h[:, off[a]:off[a] + width]
        mq, mkv, mkr = col(0, q_lora), col(1, kv_lora), col(2, rope_d)
        fq, fkv, ff = col(3, fox_h * fox_d), col(4, 2 * fox_d), col(6, fox_h)
        nq, nkv, ng = col(7, nsa_h * nsa_d), col(8, 6 * nsa_d), col(9, N_BRANCH * nsa_h)

        q = _mm(_rmsnorm(mq, g_q[l]).astype(_MXU), w_uq[l].reshape(q_lora, mla_h * mla_qd).astype(_MXU), _F32)
        q = q.reshape(-1, mla_h, mla_qd)
        q_lat = _headmm(q[..., :nope].reshape(-1, mla_h * nope).astype(_MXU),
                        jnp.transpose(w_uk[l], (1, 2, 0)).astype(_MXU), _MXU)
        q_eff = jnp.concatenate([q_lat.reshape(-1, mla_h, kv_lora),
                                 _rope(q[..., nope:], pos_all).astype(_MXU)], axis=-1)
        new_mla = jnp.concatenate([_rmsnorm(mkv, g_kv[l]), _rope(mkr, pos_all)], axis=-1)
        mla_scale = float(mla_qd) ** -0.5
        kp = new_mla[:mp].reshape(bp, tp, -1).astype(_MXU)
        o_p = _attn_dense("mla", q_eff[:mp].reshape(bp, tp * mla_h, -1), kp, kp[..., :kv_lora], mla_h,
                          scale=mla_scale, q_pos0=0, out_dtype=_MXU, tq=max(1, 1024 // mla_h))
        o_s = _attn_paged("mla", q_eff[mp:].reshape(bs, ts * mla_h, -1), cache_mla_t, l, page_table,
                          new_mla[mp:].reshape(bs, ts, -1), mla_h, dk=kv_lora + rope_d, dv=kv_lora,
                          scale=mla_scale, pos0=past, out_dtype=_MXU)
        o_lat = jnp.concatenate([o_p.reshape(mp, -1), o_s.reshape(ms, -1)], axis=0)
        o_a = _headmm(o_lat, jnp.transpose(w_uv[l], (1, 0, 2)).astype(_MXU), _MXU)

        logf = jax.nn.log_sigmoid(ff + b_f[l])
        new_fox = jnp.concatenate([fkv, logf], axis=-1)
        fox_scale = float(fox_d) ** -0.5
        fqb = fq.astype(_MXU)
        cum = jnp.cumsum(logf[:mp].reshape(bp, tp, fox_h), axis=1)
        o_p = _attn_dense("fox", fqb[:mp].reshape(bp, tp * fox_h, fox_d),
                          fkv[:mp, :fox_d].reshape(bp, tp, fox_d).astype(_MXU),
                          fkv[:mp, fox_d:].reshape(bp, tp, fox_d).astype(_MXU), fox_h, scale=fox_scale, q_pos0=0,
                          col_bias=-jnp.transpose(cum, (0, 2, 1)), out_dtype=_MXU)
        o_s = _attn_paged("fox", fqb[mp:].reshape(bs, ts * fox_h, fox_d), cache_fox_t, l, page_table,
                          new_fox[mp:].reshape(bs, ts, -1), fox_h, dk=fox_d, dv=fox_d,
                          scale=fox_scale, pos0=past, out_dtype=_MXU)
        o_b = jnp.concatenate([o_p.reshape(mp, -1), o_s.reshape(ms, -1)], axis=0)

        nsa_scale = float(nsa_d) ** -0.5
        nqb = nq.astype(_MXU)
        new_nsa = nkv[:, :4 * nsa_d]
        pe_kv = jnp.concatenate([pe_cmp_k[l], pe_cmp_v[l]], axis=-1)
        zero = jnp.zeros((blk, nsa_d, nsa_d), _F32)
        w_kv = jnp.concatenate([jnp.concatenate([w_cmp_k[l], zero], axis=-1),
                                jnp.concatenate([zero, w_cmp_v[l]], axis=-1)], axis=1)
        w_kv = w_kv.reshape(blk * 2 * nsa_d, 2 * nsa_d).astype(_MXU)
        cmp_p = _compress(new_nsa[:mp], pe_kv, w_kv, blk, 8192).reshape(bp, n_cmp_p, 2 * nsa_d)
        ncp_p = _round_up(n_cols_p, LANES)
        q_p = nqb[:mp].reshape(bp, tp * nsa_h, nsa_d)
        o_cmp_p, sel_p = _cmp_select(q_p, _pad_rows(cmp_p[..., :nsa_d], ncp_p).astype(_MXU),
                                     _pad_rows(cmp_p[..., nsa_d:], ncp_p).astype(_MXU), nsa_h, slopes,
                                     scale=nsa_scale, q_pos0=0, n_cmp=n_cmp_p, n_cols=n_cols_p, blk=blk, topn=NSA_TOPN,
                                     tq=32)
        kv_p = nkv[:mp].reshape(bp, tp, 6 * nsa_d).astype(_MXU)
        o_slc_p = _attn_dense("slc", q_p, kv_p[..., 2 * nsa_d:3 * nsa_d], kv_p[..., 3 * nsa_d:4 * nsa_d], nsa_h,
                              scale=nsa_scale, q_pos0=0, slopes=slopes, sel=sel_p, blk=blk)
        o_win_p = _attn_dense("win", q_p, kv_p[..., 4 * nsa_d:5 * nsa_d], kv_p[..., 5 * nsa_d:], nsa_h,
                              scale=nsa_scale, q_pos0=0, slopes=slopes, window=NSA_WINDOW)
        cmp_pool = _compress(cache_nsa.reshape(depth * n_pool * page, -1), pe_kv, w_kv, blk, 128 * page,
                             row0=l * n_pool * page, rows=n_pool * page)
        cmp_s = cmp_pool.reshape(n_pool, page // blk, 2 * nsa_d)[page_table].reshape(bs, -1, 2 * nsa_d)[:, :n_cmp_s]
        ncp_s = _round_up(n_cols_s, LANES)
        q_s = nqb[mp:].reshape(bs, ts * nsa_h, nsa_d)
        o_cmp_s, sel_s = _cmp_select(q_s, _pad_rows(cmp_s[..., :nsa_d], ncp_s).astype(_MXU),
                                     _pad_rows(cmp_s[..., nsa_d:], ncp_s).astype(_MXU), nsa_h, slopes,
                                     scale=nsa_scale, q_pos0=past, n_cmp=n_cmp_s, n_cols=n_cols_s, blk=blk,
                                     topn=NSA_TOPN, tq=ts)
        o_slc_s = _attn_paged("slc", q_s, cache_nsa, l, page_table, new_nsa[mp:, 2 * nsa_d:].reshape(bs, ts, -1),
                              nsa_h, dk=nsa_d, dv=nsa_d, scale=nsa_scale, pos0=past, slopes=slopes, sel=sel_s,
                              blk=blk)
        win_rows = jnp.concatenate([state_nsa_win[l], nkv[mp:, 4 * nsa_d:].reshape(bs, ts, 2 * nsa_d)], axis=1)
        band = _pad_rows(win_rows, _round_up(win_buf + ts, LANES)).astype(_MXU)
        o_win_s = _attn_dense("win", q_s, band[..., :nsa_d], band[..., nsa_d:], nsa_h, scale=nsa_scale,
                              q_pos0=past, k_pos0=past - win_buf, n_valid=win_buf + ts, slopes=slopes,
                              window=NSA_WINDOW, tq=ts)
        gate = jax.nn.sigmoid(ng).reshape(-1, N_BRANCH, nsa_h, 1)
        br = lambda p_, s_: jnp.concatenate([p_.reshape(mp, nsa_h, nsa_d), s_.reshape(ms, nsa_h, nsa_d)], axis=0)
        o_c = (gate[:, 0] * br(o_cmp_p, o_cmp_s) + gate[:, 1] * br(o_slc_p, o_slc_s)
               + gate[:, 2] * br(o_win_p, o_win_s)).reshape(-1, nsa_h * nsa_d).astype(_MXU)

        merged = _gated_proj(o_a, o_b, o_c, w_proj_a[l].astype(_MXU), w_proj_b[l].astype(_MXU),
                             w_proj_c[l].astype(_MXU), h, gate_off)
        mix = _mm(merged, w_out[l].astype(_MXU), _F32)
        x, xb = _add_ln(x, mix, ln1_g[l], ln1_b[l], alpha)

        pq = _mm(xb, w_pq[l].reshape(dm, peer_h * dkey).astype(_MXU), _MXU)
        s1, s2, st = _peer_route(pq, peer_keys[l].astype(_MXU), PEER_TOPK)
        ffn = _mm_acc(_peer_hidden(xb, peer_u[l].astype(_MXU), s1, s2, st), peer_v[l].astype(_MXU))
        x, xb = _add_ln(x, ffn, ln2_g[l], ln2_b[l], alpha)

        keep_p = min(NSA_WINDOW, tp)
        keep_s = min(NSA_WINDOW, past + ts)
        outs["mla_p"].append(new_mla[:mp].reshape(bp, tp, -1))
        outs["mla_s"].append(new_mla[mp:].reshape(bs, ts, -1))
        outs["fox_p"].append(new_fox[:mp].reshape(bp, tp, -1))
        outs["fox_s"].append(new_fox[mp:].reshape(bs, ts, -1))
        outs["nsa_p"].append(new_nsa[:mp].reshape(bp, tp, -1))
        outs["nsa_s"].append(new_nsa[mp:].reshape(bs, ts, -1))
        outs["win_p"].append(nkv[:mp, 4 * nsa_d:].reshape(bp, tp, -1)[:, tp - keep_p:])
        outs["win_s"].append(win_rows[:, win_rows.shape[1] - keep_s:])

    return (x[:mp].reshape(bp, tp, dm), x[mp:].reshape(bs, ts, dm),
            jnp.stack(outs["mla_p"]), jnp.stack(outs["mla_s"]), jnp.stack(outs["fox_p"]), jnp.stack(outs["fox_s"]),
            jnp.stack(outs["nsa_p"]), jnp.stack(outs["nsa_s"]), jnp.stack(outs["win_p"]), jnp.stack(outs["win_s"]))
```
